```python
import jax, jax.numpy as jnp
from jax import lax
import numpy as np

D_MODEL = 1024
BATCH = 32
SEQ = 2048
DEPTH = 1
DEC_BATCH = 8
DEC_SEQ = 2048
PAST_LEN = 128

MLA_HEADS = 8
MLA_Q_LORA = 256
MLA_KV_LORA = 128
MLA_NOPE = 64
MLA_ROPE = 32
MLA_V = 64
ROPE_THETA = 10000.0
Q_BLOCK = 128
GLA_HEADS = 4
GLA_DK = 64
GLA_DV = 128
GLA_GATE_RANK = 16
GLA_GATE_NORM = 16.0
GLA_CHUNK = 64
MIX_WIDTH = MLA_HEADS * MLA_V + GLA_HEADS * GLA_DV
IN_SIZES = (MLA_Q_LORA, MLA_KV_LORA, MLA_ROPE,
            GLA_HEADS * GLA_DK, GLA_HEADS * GLA_DK, GLA_HEADS * GLA_DV,
            GLA_GATE_RANK, GLA_GATE_RANK, GLA_HEADS * GLA_DV)
IN_WIDTH = sum(IN_SIZES)
IN_SPLITS = tuple(int(s) for s in np.cumsum(IN_SIZES)[:-1])
PEER_HEADS = 8
PEER_NKEYS = 128
PEER_NEXP = PEER_NKEYS * PEER_NKEYS
PEER_QDIM = 256
PEER_TOPK = 16
PEER_TOKEN_BLOCK = 128
EPS = 1e-6

kernel_name = "hymba_mla_gla_peer_encoder"


def rms_norm(x, g):
    xf = x.astype(jnp.float32)
    y = xf * lax.rsqrt(jnp.mean(xf * xf, axis=-1, keepdims=True) + EPS)
    return (y * g.astype(jnp.float32)).astype(x.dtype)


def rope_tables(seq_len):
    half = MLA_ROPE // 2
    freqs = ROPE_THETA ** (-jnp.arange(half, dtype=jnp.float32) * 2.0 / MLA_ROPE)
    ang = jnp.arange(seq_len, dtype=jnp.float32)[:, None] * freqs[None, :]
    return jnp.cos(ang), jnp.sin(ang)


def apply_rope(x, cos, sin):
    xf = x.astype(jnp.float32)
    x1, x2 = jnp.split(xf, 2, axis=-1)
    return jnp.concatenate([x1 * cos - x2 * sin, x2 * cos + x1 * sin], axis=-1).astype(x.dtype)


def mla_attention(q_nope, q_rope, k_nope, k_rope, v):
    B, S, H, _ = q_nope.shape
    nq = S // Q_BLOCK
    scale = (MLA_NOPE + MLA_ROPE) ** -0.5
    qn = q_nope.reshape(B, nq, Q_BLOCK, H, MLA_NOPE).transpose(1, 0, 2, 3, 4)
    qr = q_rope.reshape(B, nq, Q_BLOCK, H, MLA_ROPE).transpose(1, 0, 2, 3, 4)

    def block(args):
        qn_b, qr_b = args
        s = (jnp.einsum("bqhd,bkhd->bhqk", qn_b, k_nope)
             + jnp.einsum("bqhd,bkd->bhqk", qr_b, k_rope))
        p = jax.nn.softmax(s.astype(jnp.float32) * scale, axis=-1).astype(v.dtype)
        return jnp.einsum("bhqk,bkhd->bqhd", p, v)

    o = lax.map(block, (qn, qr))
    return o.transpose(1, 0, 2, 3, 4).reshape(B, S, H * MLA_V)


def gla_scan(q, k, v, g):
    B, S, H, DK = q.shape
    DV = v.shape[-1]
    n = S // GLA_CHUNK
    out_dtype = v.dtype

    def to_chunks(a):
        return a.astype(jnp.float32).reshape(B, n, GLA_CHUNK, H, a.shape[-1]).transpose(1, 0, 3, 2, 4)

    qc, kc, vc, gc = to_chunks(q), to_chunks(k), to_chunks(v), to_chunks(g)
    mask = jnp.tril(jnp.ones((GLA_CHUNK, GLA_CHUNK), dtype=bool))

    def step(state, inp):
        qi, ki, vi, gi = inp
        b = jnp.cumsum(gi, axis=-2)
        q_dec = qi * jnp.exp(b)
        att = jnp.einsum("bhid,bhjd->bhij", q_dec, ki * jnp.exp(-b))
        att = jnp.where(mask, att, 0.0)
        o = jnp.einsum("bhij,bhjv->bhiv", att, vi) + jnp.einsum("bhid,bhdv->bhiv", q_dec, state)
        b_last = b[..., -1:, :]
        state = (state * jnp.exp(b_last)[..., 0, :, None]
                 + jnp.einsum("bhjd,bhjv->bhdv", ki * jnp.exp(b_last - b), vi))
        return state, o

    state0 = jnp.zeros((B, H, DK, DV), jnp.float32)
    _, o = lax.scan(step, state0, (qc, kc, vc, gc))
    return o.transpose(1, 0, 3, 2, 4).reshape(B, S, H, DV).astype(out_dtype)


def peer(xn, w_q, sub_keys, u_tab, v_tab):
    B, S, D = xn.shape
    xt = xn.reshape(-1, PEER_TOKEN_BLOCK, D)

    def block(xb):
        T = xb.shape[0]
        q = (xb @ w_q).reshape(T, PEER_HEADS, 2, PEER_QDIM // 2)
        s = jnp.einsum("thpd,hpkd->thpk", q, sub_keys).astype(jnp.float32)
        s1, i1 = lax.top_k(s[:, :, 0], PEER_TOPK)
        s2, i2 = lax.top_k(s[:, :, 1], PEER_TOPK)
        cand = (s1[..., :, None] + s2[..., None, :]).reshape(T, PEER_HEADS, PEER_TOPK * PEER_TOPK)
        cidx = (i1[..., :, None] * PEER_NKEYS + i2[..., None, :]).reshape(T, PEER_HEADS, PEER_TOPK * PEER_TOPK)
        top_s, pos = lax.top_k(cand, PEER_TOPK)
        idx = jnp.take_along_axis(cidx, pos, axis=-1)
        gate = jax.nn.softmax(top_s, axis=-1)
        u = u_tab[idx]
        vv = v_tab[idx]
        act = jax.nn.gelu(jnp.einsum("thkd,td->thk", u, xb).astype(jnp.float32))
        w = (gate * act).astype(xb.dtype)
        return jnp.einsum("thk,thkd->td", w, vv)

    return lax.map(block, xt).reshape(B, S, D)


def encoder_layer(x, norm1_g, w_in, q_norm_g, w_uq, kv_norm_g, w_ukv,
                  gate_fwd_w, gate_fwd_b, gate_bwd_w, gate_bwd_b, gla_norm_g, w_o,
                  norm2_g, peer_wq, peer_subkeys, peer_u, peer_v):
    B, S, D = x.shape
    n1 = rms_norm(x, norm1_g)
    proj = n1 @ w_in
    c_q, c_kv, k_r, gq, gk, gv, glr_f, glr_b, gout = jnp.split(proj, IN_SPLITS, axis=-1)

    qf = (rms_norm(c_q, q_norm_g) @ w_uq).reshape(B, S, MLA_HEADS, MLA_NOPE + MLA_ROPE)
    q_nope, q_rope = qf[..., :MLA_NOPE], qf[..., MLA_NOPE:]
    kvf = (rms_norm(c_kv, kv_norm_g) @ w_ukv).reshape(B, S, MLA_HEADS, MLA_NOPE + MLA_V)
    k_nope, v_mla = kvf[..., :MLA_NOPE], kvf[..., MLA_NOPE:]
    cos, sin = rope_tables(S)
    q_rope = apply_rope(q_rope, cos[:, None, :], sin[:, None, :])
    k_rope = apply_rope(k_r, cos, sin)
    attn = mla_attention(q_nope, q_rope, k_nope, k_rope, v_mla)

    qg = gq.reshape(B, S, GLA_HEADS, GLA_DK) * (GLA_DK ** -0.5)
    kg = gk.reshape(B, S, GLA_HEADS, GLA_DK)
    vg = gv.reshape(B, S, GLA_HEADS, GLA_DV)
    g_f = (jax.nn.log_sigmoid((glr_f @ gate_fwd_w + gate_fwd_b).astype(jnp.float32))
           / GLA_GATE_NORM).reshape(B, S, GLA_HEADS, GLA_DK)
    g_b = (jax.nn.log_sigmoid((glr_b @ gate_bwd_w + gate_bwd_b).astype(jnp.float32))
           / GLA_GATE_NORM).reshape(B, S, GLA_HEADS, GLA_DK)
    o_f = gla_scan(qg, kg, vg, g_f)
    o_b = jnp.flip(gla_scan(jnp.flip(qg, 1), jnp.flip(kg, 1), jnp.flip(vg, 1), jnp.flip(g_b, 1)), 1)
    o_gla = rms_norm(o_f + o_b, gla_norm_g) * jax.nn.silu(gout.reshape(B, S, GLA_HEADS, GLA_DV))

    mix = jnp.concatenate([attn, o_gla.reshape(B, S, GLA_HEADS * GLA_DV)], axis=-1) @ w_o
    h = x + mix
    return h + peer(rms_norm(h, norm2_g), peer_wq, peer_subkeys, peer_u, peer_v)


def setup_inputs(seed: int = 0) -> dict:
    key = jax.random.key(seed)
    ks = jax.random.split(key, 24)
    f32 = jnp.float32

    def nrm(k, shape, scale):
        return jax.random.normal(k, shape, f32) * scale

    def gain(k, shape):
        return 1.0 + 0.02 * jax.random.normal(k, shape, f32)

    L = DEPTH
    return {
        "x_prompt": jax.random.normal(ks[0], (BATCH, SEQ, D_MODEL), f32),
        "x_sample": jax.random.normal(ks[1], (DEC_BATCH, DEC_SEQ, D_MODEL), f32),
        "norm1_g": gain(ks[2], (L, D_MODEL)),
        "w_in": nrm(ks[3], (L, D_MODEL, IN_WIDTH), D_MODEL ** -0.5),
        "q_norm_g": gain(ks[4], (L, MLA_Q_LORA)),
        "w_uq": nrm(ks[5], (L, MLA_Q_LORA, MLA_HEADS * (MLA_NOPE + MLA_ROPE)), MLA_Q_LORA ** -0.5),
        "kv_norm_g": gain(ks[6], (L, MLA_KV_LORA)),
        "w_ukv": nrm(ks[7], (L, MLA_KV_LORA, MLA_HEADS * (MLA_NOPE + MLA_V)), MLA_KV_LORA ** -0.5),
        "gate_fwd_w": nrm(ks[8], (L, GLA_GATE_RANK, GLA_HEADS * GLA_DK), GLA_GATE_RANK ** -0.5),
        "gate_fwd_b": nrm(ks[9], (L, GLA_HEADS * GLA_DK), 0.1),
        "gate_bwd_w": nrm(ks[10], (L, GLA_GATE_RANK, GLA_HEADS * GLA_DK), GLA_GATE_RANK ** -0.5),
        "gate_bwd_b": nrm(ks[11], (L, GLA_HEADS * GLA_DK), 0.1),
        "gla_norm_g": gain(ks[12], (L, GLA_DV)),
        "w_o": nrm(ks[13], (L, MIX_WIDTH, D_MODEL), MIX_WIDTH ** -0.5),
        "norm2_g": gain(ks[14], (L, D_MODEL)),
        "peer_wq": nrm(ks[15], (L, D_MODEL, PEER_HEADS * PEER_QDIM), D_MODEL ** -0.5),
        "peer_subkeys": nrm(ks[16], (L, PEER_HEADS, 2, PEER_NKEYS, PEER_QDIM // 2), (PEER_QDIM // 2) ** -0.5),
        "peer_u": nrm(ks[17], (L, PEER_NEXP, D_MODEL), D_MODEL ** -0.5),
        "peer_v": nrm(ks[18], (L, PEER_NEXP, D_MODEL), (PEER_HEADS * PEER_TOPK) ** -0.5),
        "final_norm_g": gain(ks[19], (D_MODEL,)),
    }


def reference(x_prompt, x_sample, norm1_g, w_in, q_norm_g, w_uq, kv_norm_g, w_ukv,
              gate_fwd_w, gate_fwd_b, gate_bwd_w, gate_bwd_b, gla_norm_g, w_o,
              norm2_g, peer_wq, peer_subkeys, peer_u, peer_v, final_norm_g):
    def trunk(x):
        for l in range(DEPTH):
            x = encoder_layer(x, norm1_g[l], w_in[l], q_norm_g[l], w_uq[l], kv_norm_g[l], w_ukv[l],
                              gate_fwd_w[l], gate_fwd_b[l], gate_bwd_w[l], gate_bwd_b[l],
                              gla_norm_g[l], w_o[l], norm2_g[l], peer_wq[l], peer_subkeys[l],
                              peer_u[l], peer_v[l])
        return rms_norm(x, final_norm_g)

    y_prompt = trunk(x_prompt)
    y_sample = trunk(x_sample)
    return (y_prompt, y_sample)
```

```python
import functools

import jax
import jax.numpy as jnp
from jax import lax
from jax.experimental import pallas as pl
from jax.experimental.pallas import tpu as pltpu

D_MODEL = 1024
MLA_HEADS = 8
MLA_Q_LORA = 256
MLA_KV_LORA = 128
MLA_NOPE = 64
MLA_ROPE = 32
MLA_V = 64
ROPE_THETA = 10000.0
GLA_HEADS = 4
GLA_DK = 64
GLA_DV = 128
GLA_GATE_RANK = 16
GLA_GATE_NORM = 16.0
GLA_CHUNK = 64
PEER_HEADS = 8
PEER_NKEYS = 128
PEER_NEXP = PEER_NKEYS * PEER_NKEYS
PEER_QDIM = 256
PEER_TOPK = 16
EPS = 1e-6

LANES = 128
HEAD_PAD = LANES
VMEM_LIMIT_BYTES = 56 * 1024 * 1024

_C_Q = 0
_C_KV = _C_Q + MLA_Q_LORA
_C_GQ = _C_KV + MLA_KV_LORA
_C_GK = _C_GQ + GLA_HEADS * HEAD_PAD
_C_GV = _C_GK + GLA_HEADS * HEAD_PAD
_C_GO = _C_GV + GLA_HEADS * GLA_DV
_C_MISC = _C_GO + GLA_HEADS * GLA_DV
_IN_COLS = _C_MISC + LANES
_M_KR = 0
_M_KROT = MLA_ROPE
_M_GF = 2 * MLA_ROPE
_M_GB = _M_GF + GLA_GATE_RANK

BF16 = jnp.bfloat16
F32 = jnp.float32


def _dot(a, b):
    return jnp.dot(a, b, preferred_element_type=F32)


def _dot_nt(a, b):
    return lax.dot_general(a, b, (((1,), (1,)), ((), ())), preferred_element_type=F32)


def _dot_tn(a, b):
    return lax.dot_general(a, b, (((0,), (0,)), ((), ())), preferred_element_type=F32)


def _rms(x, g):
    return x * lax.rsqrt(jnp.mean(x * x, axis=-1, keepdims=True) + EPS) * g


def _log_sigmoid(x):
    return jnp.minimum(x, 0.0) - jnp.log(1.0 + jnp.exp(-jnp.abs(x)))


def _gelu_tanh(x):
    c = 0.7978845608028654
    return 0.5 * x * (1.0 + jnp.tanh(c * (x + 0.044715 * (x * x * x))))


def _params(*sem):
    return pltpu.CompilerParams(dimension_semantics=sem, vmem_limit_bytes=VMEM_LIMIT_BYTES)


def _full(shape):
    nd = len(shape)
    return pl.BlockSpec(shape, lambda *_: (0,) * nd)


def _proj_kernel(x_ref, g1_ref, w1_ref, gq_ref, gkv_ref, wqa_ref, wqb_ref, wka_ref, wv_ref,
                 pa_ref, pb_ref, gfw_ref, gfb_ref, gbw_ref, gbb_ref,
                 cq_ref, sq_ref, ck_ref, sk_ref,
                 q_ref, k_ref, v_ref, gq_o, gk_o, gv_o, gf_o, gb_o, sg_o):
    x = x_ref[...]
    n1 = _rms(x, g1_ref[...]).astype(BF16)
    p = _dot(n1, w1_ref[...])
    cqn = _rms(p[:, _C_Q:_C_Q + MLA_Q_LORA], gq_ref[...]).astype(BF16)
    ckvn = _rms(p[:, _C_KV:_C_KV + MLA_KV_LORA], gkv_ref[...]).astype(BF16)
    misc = p[:, _C_MISC:_C_MISC + LANES].astype(BF16)

    qa = _dot(cqn, wqa_ref[...])
    qb = _dot(cqn, wqb_ref[...])
    ka = _dot(ckvn, wka_ref[...]) + _dot(misc, pa_ref[...])
    kb = _dot(misc, pb_ref[...])
    cq, sq, ck, sk = cq_ref[...], sq_ref[...], ck_ref[...], sk_ref[...]
    for h in range(MLA_HEADS):
        sl = slice(h * HEAD_PAD, (h + 1) * HEAD_PAD)
        q_ref[:, sl] = (qa[:, sl] * cq + qb[:, sl] * sq).astype(BF16)
        k_ref[:, sl] = (ka[:, sl] * ck + kb[:, sl] * sk).astype(BF16)
    v_ref[...] = _dot(ckvn, wv_ref[...]).astype(BF16)

    w = GLA_HEADS * HEAD_PAD
    gq_o[...] = p[:, _C_GQ:_C_GQ + w].astype(BF16)
    gk_o[...] = p[:, _C_GK:_C_GK + w].astype(BF16)
    gv_o[...] = p[:, _C_GV:_C_GV + GLA_HEADS * GLA_DV].astype(BF16)
    go = p[:, _C_GO:_C_GO + GLA_HEADS * GLA_DV]
    sg_o[...] = (go * (1.0 / (1.0 + jnp.exp(-go)))).astype(BF16)
    gf_o[...] = _log_sigmoid(_dot(misc, gfw_ref[...]) + gfb_ref[...]) * (1.0 / GLA_GATE_NORM)
    gb_o[...] = _log_sigmoid(_dot(misc, gbw_ref[...]) + gbb_ref[...]) * (1.0 / GLA_GATE_NORM)


def _proj(x2, wp, seq, tm):
    n = x2.shape[0]
    tiles_per_seq = seq // tm
    tok = lambda w: pl.BlockSpec((tm, w), lambda i: (i, 0))
    tab = pl.BlockSpec((tm, HEAD_PAD), lambda i: (i % tiles_per_seq, 0))
    weights = [wp["g1"], wp["w1"], wp["gq"], wp["gkv"], wp["wqa"], wp["wqb"], wp["wka"], wp["wv"],
               wp["pa"], wp["pb"], wp["gfw"], wp["gfb"], wp["gbw"], wp["gbb"]]
    tables = [wp["cq"], wp["sq"], wp["ck"], wp["sk"]]
    hp = MLA_HEADS * HEAD_PAD
    gw = GLA_HEADS * HEAD_PAD
    gv = GLA_HEADS * GLA_DV
    out_w = [(hp, BF16), (hp, BF16), (hp, BF16), (gw, BF16), (gw, BF16), (gv, BF16),
             (gw, F32), (gw, F32), (gv, BF16)]
    return pl.pallas_call(
        _proj_kernel,
        grid=(n // tm,),
        in_specs=[tok(D_MODEL)] + [_full(w.shape) for w in weights] + [tab] * 4,
        out_specs=[tok(w) for w, _ in out_w],
        out_shape=[jax.ShapeDtypeStruct((n, w), dt) for w, dt in out_w],
        compiler_params=_params("parallel"),
        name="proj",
    )(x2, *weights, *tables)


def _attn_kernel(q_ref, k_ref, v_ref, o_ref):
    s = _dot_nt(q_ref[...], k_ref[...])
    m = jnp.max(s, axis=-1, keepdims=True)
    p = jnp.exp(s - m)
    l = jnp.sum(p, axis=-1, keepdims=True)
    o = _dot(p.astype(BF16), v_ref[...])
    o_ref[...] = (o * (1.0 / l)).astype(BF16)


def _attn(q, k, v, seq, tq):
    n = q.shape[0]
    nb = n // seq
    nq = seq // tq
    return pl.pallas_call(
        _attn_kernel,
        grid=(nb, MLA_HEADS, nq),
        in_specs=[pl.BlockSpec((tq, HEAD_PAD), lambda b, h, i: (b * nq + i, h)),
                  pl.BlockSpec((seq, HEAD_PAD), lambda b, h, i: (b, h)),
                  pl.BlockSpec((seq, HEAD_PAD), lambda b, h, i: (b, h))],
        out_specs=pl.BlockSpec((tq, HEAD_PAD), lambda b, h, i: (b * nq + i, h)),
        out_shape=jax.ShapeDtypeStruct((n, MLA_HEADS * HEAD_PAD), BF16),
        compiler_params=_params("parallel", "parallel", "arbitrary"),
        name="attn",
    )(q, k, v)


def _gla_kernel(q_ref, k_ref, v_ref, gf_ref, gb_ref, sg_ref, gn_ref, o_ref, of_ref, ob_ref, *, seq):
    c = GLA_CHUNK
    nchunk = seq // c
    row = lax.broadcasted_iota(jnp.int32, (c, c), 0)
    col = lax.broadcasted_iota(jnp.int32, (c, c), 1)
    lower = row >= col
    tri_f = jnp.where(lower, 1.0, 0.0).astype(BF16)
    tri_b = jnp.where(row <= col, 1.0, 0.0).astype(BF16)

    def cumsum(tri, g):
        hi = g.astype(BF16)
        lo = (g - hi.astype(F32)).astype(BF16)
        return _dot(tri, hi) + _dot(tri, lo)

    def chunk(rows, g_ref, tri, causal, edge, state_t):
        q = q_ref[rows, :].astype(F32)
        k = k_ref[rows, :].astype(F32)
        v = v_ref[rows, :]
        b = cumsum(tri, g_ref[rows, :])
        b_edge = b[edge:edge + 1, :]
        q_dec = (q * jnp.exp(b)).astype(BF16)
        att = _dot_nt(q_dec, (k * jnp.exp(-b)).astype(BF16))
        att = jnp.where(causal, att, 0.0).astype(BF16)
        o = _dot(att, v) + _dot_nt(q_dec, state_t.astype(BF16))
        k_dec = (k * jnp.exp(b_edge - b)).astype(BF16)
        state_t = state_t * jnp.exp(b_edge) + _dot_tn(v, k_dec)
        return o, state_t

    def body(i, carry):
        sf, sb = carry
        rf = pl.ds(pl.multiple_of(i * c, c), c)
        rb = pl.ds(pl.multiple_of((nchunk - 1 - i) * c, c), c)
        o_f, sf = chunk(rf, gf_ref, tri_f, lower, c - 1, sf)
        o_b, sb = chunk(rb, gb_ref, tri_b, row <= col, 0, sb)
        of_ref[rf, :] = o_f
        ob_ref[rb, :] = o_b
        return sf, sb

    zero = jnp.zeros((GLA_DV, HEAD_PAD), F32)
    lax.fori_loop(0, nchunk, body, (zero, zero))
    o = of_ref[...] + ob_ref[...]
    o_ref[...] = (_rms(o, gn_ref[...]) * sg_ref[...].astype(F32)).astype(BF16)


def _gla(gq, gk, gv, gf, gb, sg, gn, seq):
    n = gq.shape[0]
    nb = n // seq
    blk = lambda w: pl.BlockSpec((seq, w), lambda b, h: (b, h))
    return pl.pallas_call(
        functools.partial(_gla_kernel, seq=seq),
        grid=(nb, GLA_HEADS),
        in_specs=[blk(HEAD_PAD), blk(HEAD_PAD), blk(GLA_DV), blk(HEAD_PAD), blk(HEAD_PAD), blk(GLA_DV),
                  _full(gn.shape)],
        out_specs=blk(GLA_DV),
        out_shape=jax.ShapeDtypeStruct((n, GLA_HEADS * GLA_DV), BF16),
        scratch_shapes=[pltpu.VMEM((seq, GLA_DV), F32), pltpu.VMEM((seq, GLA_DV), F32)],
        compiler_params=_params("parallel", "parallel"),
        name="gla",
    )(gq, gk, gv, gf, gb, sg, gn)


def _mix_kernel(x_ref, a_ref, g_ref, woa_ref, wob_ref, g2_ref, h_ref, xn_ref):
    h = x_ref[...] + _dot(a_ref[...], woa_ref[...]) + _dot(g_ref[...], wob_ref[...])
    h_ref[...] = h
    xn_ref[...] = _rms(h, g2_ref[...]).astype(BF16)


def _mix(x2, attn, ogla, wp, tm):
    n = x2.shape[0]
    tok = lambda w: pl.BlockSpec((tm, w), lambda i: (i, 0))
    return pl.pallas_call(
        _mix_kernel,
        grid=(n // tm,),
        in_specs=[tok(D_MODEL), tok(attn.shape[1]), tok(ogla.shape[1]),
                  _full(wp["woa"].shape), _full(wp["wob"].shape), _full(wp["g2"].shape)],
        out_specs=[tok(D_MODEL), tok(D_MODEL)],
        out_shape=[jax.ShapeDtypeStruct((n, D_MODEL), F32), jax.ShapeDtypeStruct((n, D_MODEL), BF16)],
        compiler_params=_params("parallel"),
        name="mix",
    )(x2, attn, ogla, wp["woa"], wp["wob"], wp["g2"])


def _top_values(s, count):
    cur = jnp.full((1, s.shape[1]), jnp.inf, F32)
    vals = []
    for _ in range(count):
        cur = jnp.max(jnp.where(s < cur, s, -jnp.inf), axis=0, keepdims=True)
        vals.append(cur)
    return vals


def _select_kernel(xn_ref, wqt_ref, sk_ref, s1_ref, s2_ref, e1_ref, e2_ref, thr_ref):
    qt = _dot_nt(wqt_ref[...], xn_ref[...])
    half = PEER_QDIM // 2
    for h in range(PEER_HEADS):
        s, tops = [], []
        for p in range(2):
            r0 = (2 * h + p) * half
            sp = _dot(sk_ref[2 * h + p], qt[r0:r0 + half, :].astype(BF16))
            s.append(sp)
            tops.append(_top_values(sp, PEER_TOPK))
        cand = jnp.concatenate([tops[0][a] + jnp.concatenate(tops[1], axis=0)
                                for a in range(PEER_TOPK)], axis=0)
        thr = _top_values(cand, PEER_TOPK)[-1]
        best = tops[0][0] + tops[1][0]
        z = jnp.sum(jnp.where(cand >= thr, jnp.exp(cand - best), 0.0), axis=0, keepdims=True)
        s1_ref[h] = s[0]
        s2_ref[h] = s[1]
        e1_ref[h] = jnp.exp(s[0] - tops[0][0])
        e2_ref[h] = jnp.exp(s[1] - tops[1][0]) * (1.0 / z)
        thr_ref[h:h + 1, :] = thr


def _select(xn, wp, tb):
    n = xn.shape[0]
    blk = pl.BlockSpec((PEER_HEADS, PEER_NKEYS, tb), lambda i: (0, 0, i))
    shp = jax.ShapeDtypeStruct((PEER_HEADS, PEER_NKEYS, n), F32)
    return pl.pallas_call(
        _select_kernel,
        grid=(n // tb,),
        in_specs=[pl.BlockSpec((tb, D_MODEL), lambda i: (i, 0)),
                  _full(wp["wqt"].shape), _full(wp["subk"].shape)],
        out_specs=[blk, blk, blk, blk, pl.BlockSpec((PEER_HEADS, tb), lambda i: (0, i))],
        out_shape=[shp, shp, shp, shp, jax.ShapeDtypeStruct((PEER_HEADS, n), F32)],
        compiler_params=_params("parallel"),
        name="peer_select",
    )(xn, wp["wqt"], wp["subk"])


def _peer_kernel(xn_ref, h_ref, s1_ref, e1_ref, s2_ref, e2_ref, thr_ref, u_ref, vt_ref, gfin_ref,
                 o_ref, acc_ref, g_ref, *, rows_per_chunk):
    c = pl.program_id(1)

    @pl.when(c == 0)
    def _():
        acc_ref[...] = jnp.zeros_like(acc_ref)

    act = _gelu_tanh(_dot_nt(u_ref[...], xn_ref[...]))
    for j in range(rows_per_chunk):
        w = None
        for h in range(PEER_HEADS):
            total = s1_ref[h, j:j + 1, :] + s2_ref[h]
            term = jnp.where(total >= thr_ref[h:h + 1, :], e2_ref[h], 0.0) * e1_ref[h, j:j + 1, :]
            w = term if w is None else w + term
        rows = slice(j * PEER_NKEYS, (j + 1) * PEER_NKEYS)
        g_ref[rows, :] = (act[rows, :] * w).astype(BF16)
    acc_ref[...] += _dot(vt_ref[...], g_ref[...])

    @pl.when(c == pl.num_programs(1) - 1)
    def _():
        y = h_ref[...] + acc_ref[...].T
        o_ref[...] = _rms(y, gfin_ref[...])


def _peer(xn, h, sel, wp, tb, ec):
    n = xn.shape[0]
    s1, s2, e1, e2, thr = sel
    rpc = ec // PEER_NKEYS
    tok = lambda: pl.BlockSpec((tb, D_MODEL), lambda i, c: (i, 0))
    part = pl.BlockSpec((PEER_HEADS, rpc, tb), lambda i, c: (0, c, i))
    whole = pl.BlockSpec((PEER_HEADS, PEER_NKEYS, tb), lambda i, c: (0, 0, i))
    return pl.pallas_call(
        functools.partial(_peer_kernel, rows_per_chunk=rpc),
        grid=(n // tb, PEER_NEXP // ec),
        in_specs=[tok(), tok(), part, part, whole, whole,
                  pl.BlockSpec((PEER_HEADS, tb), lambda i, c: (0, i)),
                  pl.BlockSpec((ec, D_MODEL), lambda i, c: (c, 0)),
                  pl.BlockSpec((D_MODEL, ec), lambda i, c: (0, c)),
                  _full(wp["gfin"].shape)],
        out_specs=tok(),
        out_shape=jax.ShapeDtypeStruct((n, D_MODEL), F32),
        scratch_shapes=[pltpu.VMEM((D_MODEL, tb), F32), pltpu.VMEM((ec, tb), BF16)],
        compiler_params=_params("parallel", "arbitrary"),
        name="peer_dense",
    )(xn, h, s1, e1, s2, e2, thr, wp["u"], wp["vt"], wp["gfin"])


def _pad_heads(w, heads, width):
    r = w.shape[0]
    w = w.reshape(r, heads, width)
    return jnp.pad(w, ((0, 0), (0, 0), (0, HEAD_PAD - width))).reshape(r, heads * HEAD_PAD)


def _rot_cols(w):
    half = MLA_ROPE // 2
    return jnp.concatenate([-w[..., half:], w[..., :half]], axis=-1)


def _prep(seq, norm1_g, w_in, q_norm_g, w_uq, kv_norm_g, w_ukv, gate_fwd_w, gate_fwd_b,
          gate_bwd_w, gate_bwd_b, gla_norm_g, w_o, norm2_g, peer_wq, peer_subkeys, peer_u, peer_v,
          final_norm_g):
    row = lambda g: g.reshape(1, -1).astype(F32)
    o = 0
    segs = {}
    for name, width in (("cq", MLA_Q_LORA), ("ckv", MLA_KV_LORA), ("kr", MLA_ROPE),
                        ("gq", GLA_HEADS * GLA_DK), ("gk", GLA_HEADS * GLA_DK), ("gv", GLA_HEADS * GLA_DV),
                        ("lf", GLA_GATE_RANK), ("lb", GLA_GATE_RANK), ("go", GLA_HEADS * GLA_DV)):
        segs[name] = w_in[:, o:o + width]
        o += width
    misc = jnp.concatenate([segs["kr"], _rot_cols(segs["kr"]), segs["lf"], segs["lb"],
                            jnp.zeros((D_MODEL, LANES - _M_GB - GLA_GATE_RANK), F32)], axis=1)
    w1 = jnp.concatenate([segs["cq"], segs["ckv"],
                          _pad_heads(segs["gq"] * (GLA_DK ** -0.5), GLA_HEADS, GLA_DK),
                          _pad_heads(segs["gk"], GLA_HEADS, GLA_DK),
                          segs["gv"], segs["go"], misc], axis=1)

    wq = w_uq.reshape(MLA_Q_LORA, MLA_HEADS, MLA_NOPE + MLA_ROPE)
    zq = jnp.zeros((MLA_Q_LORA, MLA_HEADS, HEAD_PAD - MLA_NOPE - MLA_ROPE), F32)
    wqa = jnp.concatenate([wq, zq], axis=-1).reshape(MLA_Q_LORA, -1)
    wqb = jnp.concatenate([jnp.zeros_like(wq[..., :MLA_NOPE]), _rot_cols(wq[..., MLA_NOPE:]), zq],
                          axis=-1).reshape(MLA_Q_LORA, -1)
    wkv = w_ukv.reshape(MLA_KV_LORA, MLA_HEADS, MLA_NOPE + MLA_V)
    wka = _pad_heads(wkv[..., :MLA_NOPE].reshape(MLA_KV_LORA, -1), MLA_HEADS, MLA_NOPE)
    wv = _pad_heads(wkv[..., MLA_NOPE:].reshape(MLA_KV_LORA, -1), MLA_HEADS, MLA_V)

    lane = jnp.arange(MLA_HEADS * HEAD_PAD) % HEAD_PAD
    src = jnp.arange(LANES)[:, None]
    in_rope = (lane >= MLA_NOPE) & (lane < MLA_NOPE + MLA_ROPE)
    pa = ((src == (lane - MLA_NOPE + _M_KR)[None, :]) & in_rope[None, :]).astype(BF16)
    pb = ((src == (lane - MLA_NOPE + _M_KROT)[None, :]) & in_rope[None, :]).astype(BF16)

    def gate_w(w, b, lane0):
        wpad = _pad_heads(w, GLA_HEADS, GLA_DK)
        full = jnp.zeros((LANES, wpad.shape[1]), F32).at[lane0:lane0 + GLA_GATE_RANK].set(wpad)
        return full.astype(BF16), _pad_heads(b.reshape(1, -1), GLA_HEADS, GLA_DK)

    gfw, gfb = gate_w(gate_fwd_w, gate_fwd_b, _M_GF)
    gbw, gbb = gate_w(gate_bwd_w, gate_bwd_b, _M_GB)

    half = MLA_ROPE // 2
    freqs = ROPE_THETA ** (-jnp.arange(half, dtype=F32) * 2.0 / MLA_ROPE)
    ang = jnp.arange(seq, dtype=F32)[:, None] * freqs[None, :]
    cos, sin = jnp.cos(ang), jnp.sin(ang)
    zpad = jnp.zeros((seq, HEAD_PAD - MLA_NOPE - MLA_ROPE), F32)
    ck = jnp.concatenate([jnp.ones((seq, MLA_NOPE), F32), cos, cos, zpad], axis=1)
    sk = jnp.concatenate([jnp.zeros((seq, MLA_NOPE), F32), sin, sin, zpad], axis=1)
    scale = (MLA_NOPE + MLA_ROPE) ** -0.5

    woa = w_o[:MLA_HEADS * MLA_V].reshape(MLA_HEADS, MLA_V, D_MODEL)
    woa = jnp.pad(woa, ((0, 0), (0, HEAD_PAD - MLA_V), (0, 0))).reshape(MLA_HEADS * HEAD_PAD, D_MODEL)

    return dict(
        g1=row(norm1_g), w1=w1.astype(BF16), gq=row(q_norm_g), gkv=row(kv_norm_g),
        wqa=wqa.astype(BF16), wqb=wqb.astype(BF16), wka=wka.astype(BF16), wv=wv.astype(BF16),
        pa=pa, pb=pb, gfw=gfw, gfb=gfb, gbw=gbw, gbb=gbb,
        cq=ck * scale, sq=sk * scale, ck=ck, sk=sk,
        gn=row(gla_norm_g), woa=woa.astype(BF16), wob=w_o[MLA_HEADS * MLA_V:].astype(BF16),
        g2=row(norm2_g), wqt=peer_wq.T.astype(BF16),
        subk=peer_subkeys.reshape(2 * PEER_HEADS, PEER_NKEYS, PEER_QDIM // 2).astype(BF16),
        u=peer_u.astype(BF16), vt=peer_v.T.astype(BF16), gfin=row(final_norm_g),
    )


def _tile(n, want):
    t = min(n, want)
    assert n % t == 0, (n, want)
    return t


def _trunk(x, wp):
    b, seq, d = x.shape
    x2 = x.reshape(b * seq, d)
    n = b * seq
    tm = _tile(seq, 512)
    q, k, v, gq, gk, gv, gf, gb, sg = _proj(x2, wp, seq, tm)
    attn = _attn(q, k, v, seq, _tile(seq, 256))
    ogla = _gla(gq, gk, gv, gf, gb, sg, wp["gn"], seq)
    h, xn = _mix(x2, attn, ogla, wp, tm)
    sel = _select(xn, wp, _tile(n, 256))
    y = _peer(xn, h, sel, wp, _tile(n, 512), 1024)
    return y.reshape(b, seq, d)


def kernel(x_prompt, x_sample, norm1_g, w_in, q_norm_g, w_uq, kv_norm_g, w_ukv, gate_fwd_w, gate_fwd_b,
           gate_bwd_w, gate_bwd_b, gla_norm_g, w_o, norm2_g, peer_wq, peer_subkeys, peer_u, peer_v,
           final_norm_g):
    assert norm1_g.shape[0] == 1, "single layer trunk"
    assert x_prompt.shape[1] == x_sample.shape[1]
    wp = _prep(x_prompt.shape[1], norm1_g[0], w_in[0], q_norm_g[0], w_uq[0], kv_norm_g[0], w_ukv[0],
               gate_fwd_w[0], gate_fwd_b[0], gate_bwd_w[0], gate_bwd_b[0], gla_norm_g[0], w_o[0],
               norm2_g[0], peer_wq[0], peer_subkeys[0], peer_u[0], peer_v[0], final_norm_g)
    return _trunk(x_prompt, wp), _trunk(x_sample, wp)
```

```python
import functools

import jax
import jax.numpy as jnp
from jax import lax
from jax.experimental import pallas as pl
from jax.experimental.pallas import tpu as pltpu

D_MODEL = 1024
MLA_HEADS = 8
MLA_Q_LORA = 256
MLA_KV_LORA = 128
MLA_NOPE = 64
MLA_ROPE = 32
MLA_V = 64
ROPE_THETA = 10000.0
GLA_HEADS = 4
GLA_DK = 64
GLA_DV = 128
GLA_GATE_RANK = 16
GLA_GATE_NORM = 16.0
GLA_CHUNK = 64
PEER_HEADS = 8
PEER_NKEYS = 128
PEER_NEXP = PEER_NKEYS * PEER_NKEYS
PEER_QDIM = 256
PEER_TOPK = 16
EPS = 1e-6

LANES = 128
HEAD_PAD = LANES
VMEM_LIMIT_BYTES = 56 * 1024 * 1024

_C_Q = 0
_C_KV = _C_Q + MLA_Q_LORA
_C_GQ = _C_KV + MLA_KV_LORA
_C_GK = _C_GQ + GLA_HEADS * HEAD_PAD
_C_GV = _C_GK + GLA_HEADS * HEAD_PAD
_C_GO = _C_GV + GLA_HEADS * GLA_DV
_C_MISC = _C_GO + GLA_HEADS * GLA_DV
_IN_COLS = _C_MISC + LANES
_M_KR = 0
_M_KROT = MLA_ROPE
_M_GF = 2 * MLA_ROPE
_M_GB = _M_GF + GLA_GATE_RANK

BF16 = jnp.bfloat16
F32 = jnp.float32


def _dot(a, b):
    return jnp.dot(a, b, preferred_element_type=F32)


def _dot_nt(a, b):
    return lax.dot_general(a, b, (((1,), (1,)), ((), ())), preferred_element_type=F32)


def _dot_tn(a, b):
    return lax.dot_general(a, b, (((0,), (0,)), ((), ())), preferred_element_type=F32)


def _rms(x, g):
    return x * lax.rsqrt(jnp.mean(x * x, axis=-1, keepdims=True) + EPS) * g


def _log_sigmoid(x):
    return jnp.minimum(x, 0.0) - jnp.log(1.0 + jnp.exp(-jnp.abs(x)))


def _gelu_tanh(x):
    c = 0.7978845608028654
    return 0.5 * x * (1.0 + jnp.tanh(c * (x + 0.044715 * (x * x * x))))


def _params(*sem):
    return pltpu.CompilerParams(dimension_semantics=sem, vmem_limit_bytes=VMEM_LIMIT_BYTES)


def _full(shape):
    nd = len(shape)
    return pl.BlockSpec(shape, lambda *_: (0,) * nd)


def _proj_kernel(x_ref, g1_ref, w1_ref, gq_ref, gkv_ref, wqa_ref, wqb_ref, wka_ref, wv_ref,
                 pa_ref, pb_ref, gfw_ref, gfb_ref, gbw_ref, gbb_ref,
                 cq_ref, sq_ref, ck_ref, sk_ref,
                 q_ref, k_ref, v_ref, gq_o, gk_o, gv_o, gf_o, gb_o, sg_o):
    x = x_ref[...]
    n1 = _rms(x, g1_ref[...]).astype(BF16)
    p = _dot(n1, w1_ref[...])
    cqn = _rms(p[:, _C_Q:_C_Q + MLA_Q_LORA], gq_ref[...]).astype(BF16)
    ckvn = _rms(p[:, _C_KV:_C_KV + MLA_KV_LORA], gkv_ref[...]).astype(BF16)
    misc = p[:, _C_MISC:_C_MISC + LANES].astype(BF16)

    qa = _dot(cqn, wqa_ref[...])
    qb = _dot(cqn, wqb_ref[...])
    ka = _dot(ckvn, wka_ref[...]) + _dot(misc, pa_ref[...])
    kb = _dot(misc, pb_ref[...])
    cq, sq, ck, sk = cq_ref[...], sq_ref[...], ck_ref[...], sk_ref[...]
    for h in range(MLA_HEADS):
        sl = slice(h * HEAD_PAD, (h + 1) * HEAD_PAD)
        q_ref[:, sl] = (qa[:, sl] * cq + qb[:, sl] * sq).astype(BF16)
        k_ref[:, sl] = (ka[:, sl] * ck + kb[:, sl] * sk).astype(BF16)
    v_ref[...] = _dot(ckvn, wv_ref[...]).astype(BF16)

    w = GLA_HEADS * HEAD_PAD
    gq_o[...] = p[:, _C_GQ:_C_GQ + w].astype(BF16)
    gk_o[...] = p[:, _C_GK:_C_GK + w].astype(BF16)
    gv_o[...] = p[:, _C_GV:_C_GV + GLA_HEADS * GLA_DV].astype(BF16)
    go = p[:, _C_GO:_C_GO + GLA_HEADS * GLA_DV]
    sg_o[...] = (go * (1.0 / (1.0 + jnp.exp(-go)))).astype(BF16)
    gf_o[...] = _log_sigmoid(_dot(misc, gfw_ref[...]) + gfb_ref[...]) * (1.0 / GLA_GATE_NORM)
    gb_o[...] = _log_sigmoid(_dot(misc, gbw_ref[...]) + gbb_ref[...]) * (1.0 / GLA_GATE_NORM)


def _proj(x2, wp, seq, tm):
    n = x2.shape[0]
    tiles_per_seq = seq // tm
    tok = lambda w: pl.BlockSpec((tm, w), lambda i: (i, 0))
    tab = pl.BlockSpec((tm, HEAD_PAD), lambda i: (i % tiles_per_seq, 0))
    weights = [wp["g1"], wp["w1"], wp["gq"], wp["gkv"], wp["wqa"], wp["wqb"], wp["wka"], wp["wv"],
               wp["pa"], wp["pb"], wp["gfw"], wp["gfb"], wp["gbw"], wp["gbb"]]
    tables = [wp["cq"], wp["sq"], wp["ck"], wp["sk"]]
    hp = MLA_HEADS * HEAD_PAD
    gw = GLA_HEADS * HEAD_PAD
    gv = GLA_HEADS * GLA_DV
    out_w = [(hp, BF16), (hp, BF16), (hp, BF16), (gw, BF16), (gw, BF16), (gv, BF16),
             (gw, F32), (gw, F32), (gv, BF16)]
    return pl.pallas_call(
        _proj_kernel,
        grid=(n // tm,),
        in_specs=[tok(D_MODEL)] + [_full(w.shape) for w in weights] + [tab] * 4,
        out_specs=[tok(w) for w, _ in out_w],
        out_shape=[jax.ShapeDtypeStruct((n, w), dt) for w, dt in out_w],
        compiler_params=_params("parallel"),
        name="proj",
    )(x2, *weights, *tables)


def _attn_kernel(q_ref, k_ref, v_ref, o_ref):
    s = _dot_nt(q_ref[...], k_ref[...])
    m = jnp.max(s, axis=-1, keepdims=True)
    p = jnp.exp(s - m)
    l = jnp.sum(p, axis=-1, keepdims=True)
    o = _dot(p.astype(BF16), v_ref[...])
    o_ref[...] = (o * (1.0 / l)).astype(BF16)


def _attn(q, k, v, seq, tq):
    n = q.shape[0]
    nb = n // seq
    nq = seq // tq
    return pl.pallas_call(
        _attn_kernel,
        grid=(nb, MLA_HEADS, nq),
        in_specs=[pl.BlockSpec((tq, HEAD_PAD), lambda b, h, i: (b * nq + i, h)),
                  pl.BlockSpec((seq, HEAD_PAD), lambda b, h, i: (b, h)),
                  pl.BlockSpec((seq, HEAD_PAD), lambda b, h, i: (b, h))],
        out_specs=pl.BlockSpec((tq, HEAD_PAD), lambda b, h, i: (b * nq + i, h)),
        out_shape=jax.ShapeDtypeStruct((n, MLA_HEADS * HEAD_PAD), BF16),
        compiler_params=_params("parallel", "parallel", "arbitrary"),
        name="attn",
    )(q, k, v)


def _gla_kernel(q_ref, k_ref, v_ref, gf_ref, gb_ref, sg_ref, gn_ref, o_ref, of_ref, ob_ref, *, seq):
    c = GLA_CHUNK
    nchunk = seq // c
    row = lax.broadcasted_iota(jnp.int32, (c, c), 0)
    col = lax.broadcasted_iota(jnp.int32, (c, c), 1)
    lower = row >= col
    tri_f = jnp.where(lower, 1.0, 0.0).astype(BF16)
    tri_b = jnp.where(row <= col, 1.0, 0.0).astype(BF16)

    def cumsum(tri, g):
        hi = g.astype(BF16)
        lo = (g - hi.astype(F32)).astype(BF16)
        return _dot(tri, hi) + _dot(tri, lo)

    def chunk(rows, g_ref, tri, causal, edge, state_t):
        q = q_ref[rows, :].astype(F32)
        k = k_ref[rows, :].astype(F32)
        v = v_ref[rows, :]
        b = cumsum(tri, g_ref[rows, :])
        b_edge = b[edge:edge + 1, :]
        q_dec = (q * jnp.exp(b)).astype(BF16)
        att = _dot_nt(q_dec, (k * jnp.exp(-b)).astype(BF16))
        att = jnp.where(causal, att, 0.0).astype(BF16)
        o = _dot(att, v) + _dot_nt(q_dec, state_t.astype(BF16))
        k_dec = (k * jnp.exp(b_edge - b)).astype(BF16)
        state_t = state_t * jnp.exp(b_edge) + _dot_tn(v, k_dec)
        return o, state_t

    def body(i, carry):
        sf, sb = carry
        rf = pl.ds(pl.multiple_of(i * c, c), c)
        rb = pl.ds(pl.multiple_of((nchunk - 1 - i) * c, c), c)
        o_f, sf = chunk(rf, gf_ref, tri_f, lower, c - 1, sf)
        o_b, sb = chunk(rb, gb_ref, tri_b, row <= col, 0, sb)
        of_ref[rf, :] = o_f
        ob_ref[rb, :] = o_b
        return sf, sb

    zero = jnp.zeros((GLA_DV, HEAD_PAD), F32)
    lax.fori_loop(0, nchunk, body, (zero, zero))
    o = of_ref[...] + ob_ref[...]
    o_ref[...] = (_rms(o, gn_ref[...]) * sg_ref[...].astype(F32)).astype(BF16)


def _gla(gq, gk, gv, gf, gb, sg, gn, seq):
    n = gq.shape[0]
    nb = n // seq
    blk = lambda w: pl.BlockSpec((seq, w), lambda b, h: (b, h))
    return pl.pallas_call(
        functools.partial(_gla_kernel, seq=seq),
        grid=(nb, GLA_HEADS),
        in_specs=[blk(HEAD_PAD), blk(HEAD_PAD), blk(GLA_DV), blk(HEAD_PAD), blk(HEAD_PAD), blk(GLA_DV),
                  _full(gn.shape)],
        out_specs=blk(GLA_DV),
        out_shape=jax.ShapeDtypeStruct((n, GLA_HEADS * GLA_DV), BF16),
        scratch_shapes=[pltpu.VMEM((seq, GLA_DV), F32), pltpu.VMEM((seq, GLA_DV), F32)],
        compiler_params=_params("parallel", "parallel"),
        name="gla",
    )(gq, gk, gv, gf, gb, sg, gn)


def _mix_kernel(x_ref, a_ref, g_ref, woa_ref, wob_ref, g2_ref, h_ref, xn_ref):
    h = x_ref[...] + _dot(a_ref[...], woa_ref[...]) + _dot(g_ref[...], wob_ref[...])
    h_ref[...] = h
    xn_ref[...] = _rms(h, g2_ref[...]).astype(BF16)


def _mix(x2, attn, ogla, wp, tm):
    n = x2.shape[0]
    tok = lambda w: pl.BlockSpec((tm, w), lambda i: (i, 0))
    return pl.pallas_call(
        _mix_kernel,
        grid=(n // tm,),
        in_specs=[tok(D_MODEL), tok(attn.shape[1]), tok(ogla.shape[1]),
                  _full(wp["woa"].shape), _full(wp["wob"].shape), _full(wp["g2"].shape)],
        out_specs=[tok(D_MODEL), tok(D_MODEL)],
        out_shape=[jax.ShapeDtypeStruct((n, D_MODEL), F32), jax.ShapeDtypeStruct((n, D_MODEL), BF16)],
        compiler_params=_params("parallel"),
        name="mix",
    )(x2, attn, ogla, wp["woa"], wp["wob"], wp["g2"])


def _top_values(s, count):
    cur = jnp.full((1, s.shape[1]), jnp.inf, F32)
    vals = []
    for _ in range(count):
        cur = jnp.max(jnp.where(s < cur, s, -jnp.inf), axis=0, keepdims=True)
        vals.append(cur)
    return vals


def _dup_bf16(x):
    w = pltpu.bitcast(x.astype(BF16).astype(F32), jnp.uint32)
    return w | (w >> 16)


def _select_kernel(xn_ref, wqt_ref, sk_ref, r2_ref, e2_ref, e1_ref, cnt_ref):
    qt = _dot_nt(wqt_ref[...], xn_ref[...])
    half = PEER_QDIM // 2
    for h in range(PEER_HEADS):
        s, tops = [], []
        for p in range(2):
            r0 = (2 * h + p) * half
            sp = _dot(sk_ref[2 * h + p], qt[r0:r0 + half, :].astype(BF16))
            s.append(sp)
            tops.append(_top_values(sp, PEER_TOPK))
        top2 = jnp.concatenate(tops[1], axis=0)
        rows = [tops[0][a] + top2 for a in range(PEER_TOPK)]
        cand = jnp.concatenate(rows, axis=0)
        thr = _top_values(cand, PEER_TOPK)[-1]
        best = tops[0][0] + tops[1][0]
        z = jnp.sum(jnp.where(cand >= thr, jnp.exp(cand - best), 0.0), axis=0, keepdims=True)
        counts = [jnp.sum(jnp.where(r >= thr, 1.0, 0.0), axis=0, keepdims=True) for r in rows]
        cnt = jnp.zeros_like(s[0])
        for a in reversed(range(PEER_TOPK)):
            cnt = jnp.where(s[0] >= tops[0][a], counts[a], cnt)
        rank2 = jnp.zeros_like(s[1])
        for a in range(PEER_TOPK):
            rank2 = rank2 + jnp.where(tops[1][a] > s[1], 1.0, 0.0)
        r2_ref[h] = rank2.astype(BF16)
        e2_ref[h] = (jnp.exp(s[1] - tops[1][0]) * (1.0 / z)).astype(BF16)
        e1_ref[h] = _dup_bf16(jnp.exp(s[0] - tops[0][0]))
        cnt_ref[h] = _dup_bf16(cnt)


def _select(xn, wp, tb):
    n = xn.shape[0]
    blk = pl.BlockSpec((PEER_HEADS, PEER_NKEYS, tb), lambda i: (0, 0, i))
    shp = lambda dt: jax.ShapeDtypeStruct((PEER_HEADS, PEER_NKEYS, n), dt)
    return pl.pallas_call(
        _select_kernel,
        grid=(n // tb,),
        in_specs=[pl.BlockSpec((tb, D_MODEL), lambda i: (i, 0)),
                  _full(wp["wqt"].shape), _full(wp["subk"].shape)],
        out_specs=[blk, blk, blk, blk],
        out_shape=[shp(BF16), shp(BF16), shp(jnp.uint32), shp(jnp.uint32)],
        compiler_params=_params("parallel"),
        name="peer_select",
    )(xn, wp["wqt"], wp["subk"])


PACK = 16


def _peer_kernel(xn_ref, h_ref, r2_ref, e2_ref, e1_ref, cnt_ref, u_ref, vt_ref, gfin_ref,
                 o_ref, acc_ref, g_ref, *, rows_per_chunk):
    c = pl.program_id(1)
    tb = xn_ref.shape[0]

    @pl.when(c == 0)
    def _():
        acc_ref[...] = jnp.zeros_like(acc_ref)

    act = _gelu_tanh(_dot_nt(u_ref[...], xn_ref[...]))

    def packed_row(ref, h, j):
        word = jnp.broadcast_to(ref[h, j:j + 1, :], (PACK // 2, tb))
        return pltpu.bitcast(word, BF16)

    for j in range(rows_per_chunk):
        e1 = [packed_row(e1_ref, h, j) for h in range(PEER_HEADS)]
        cnt = [packed_row(cnt_ref, h, j) for h in range(PEER_HEADS)]
        for m in range(PEER_NKEYS // PACK):
            keys = slice(m * PACK, (m + 1) * PACK)
            w = None
            for h in range(PEER_HEADS):
                term = jnp.where(r2_ref[h, keys, :] < cnt[h], e2_ref[h, keys, :], 0.0) * e1[h]
                w = term if w is None else w + term
            rows = slice(j * PEER_NKEYS + m * PACK, j * PEER_NKEYS + (m + 1) * PACK)
            g_ref[rows, :] = act[rows, :].astype(BF16) * w
    acc_ref[...] += _dot(vt_ref[...], g_ref[...])

    @pl.when(c == pl.num_programs(1) - 1)
    def _():
        y = h_ref[...] + acc_ref[...].T
        o_ref[...] = _rms(y, gfin_ref[...])


def _peer(xn, h, sel, wp, tb, ec):
    n = xn.shape[0]
    r2, e2, e1, cnt = sel
    rpc = ec // PEER_NKEYS
    tok = lambda: pl.BlockSpec((tb, D_MODEL), lambda i, c: (i, 0))
    part = pl.BlockSpec((PEER_HEADS, rpc, tb), lambda i, c: (0, c, i))
    whole = pl.BlockSpec((PEER_HEADS, PEER_NKEYS, tb), lambda i, c: (0, 0, i))
    return pl.pallas_call(
        functools.partial(_peer_kernel, rows_per_chunk=rpc),
        grid=(n // tb, PEER_NEXP // ec),
        in_specs=[tok(), tok(), whole, whole, part, part,
                  pl.BlockSpec((ec, D_MODEL), lambda i, c: (c, 0)),
                  pl.BlockSpec((D_MODEL, ec), lambda i, c: (0, c)),
                  _full(wp["gfin"].shape)],
        out_specs=tok(),
        out_shape=jax.ShapeDtypeStruct((n, D_MODEL), F32),
        scratch_shapes=[pltpu.VMEM((D_MODEL, tb), F32), pltpu.VMEM((ec, tb), BF16)],
        compiler_params=_params("parallel", "arbitrary"),
        name="peer_dense",
    )(xn, h, r2, e2, e1, cnt, wp["u"], wp["vt"], wp["gfin"])


def _pad_heads(w, heads, width):
    r = w.shape[0]
    w = w.reshape(r, heads, width)
    return jnp.pad(w, ((0, 0), (0, 0), (0, HEAD_PAD - width))).reshape(r, heads * HEAD_PAD)


def _rot_cols(w):
    half = MLA_ROPE // 2
    return jnp.concatenate([-w[..., half:], w[..., :half]], axis=-1)


def _prep(seq, norm1_g, w_in, q_norm_g, w_uq, kv_norm_g, w_ukv, gate_fwd_w, gate_fwd_b,
          gate_bwd_w, gate_bwd_b, gla_norm_g, w_o, norm2_g, peer_wq, peer_subkeys, peer_u, peer_v,
          final_norm_g):
    row = lambda g: g.reshape(1, -1).astype(F32)
    o = 0
    segs = {}
    for name, width in (("cq", MLA_Q_LORA), ("ckv", MLA_KV_LORA), ("kr", MLA_ROPE),
                        ("gq", GLA_HEADS * GLA_DK), ("gk", GLA_HEADS * GLA_DK), ("gv", GLA_HEADS * GLA_DV),
                        ("lf", GLA_GATE_RANK), ("lb", GLA_GATE_RANK), ("go", GLA_HEADS * GLA_DV)):
        segs[name] = w_in[:, o:o + width]
        o += width
    misc = jnp.concatenate([segs["kr"], _rot_cols(segs["kr"]), segs["lf"], segs["lb"],
                            jnp.zeros((D_MODEL, LANES - _M_GB - GLA_GATE_RANK), F32)], axis=1)
    w1 = jnp.concatenate([segs["cq"], segs["ckv"],
                          _pad_heads(segs["gq"] * (GLA_DK ** -0.5), GLA_HEADS, GLA_DK),
                          _pad_heads(segs["gk"], GLA_HEADS, GLA_DK),
                          segs["gv"], segs["go"], misc], axis=1)

    wq = w_uq.reshape(MLA_Q_LORA, MLA_HEADS, MLA_NOPE + MLA_ROPE)
    zq = jnp.zeros((MLA_Q_LORA, MLA_HEADS, HEAD_PAD - MLA_NOPE - MLA_ROPE), F32)
    wqa = jnp.concatenate([wq, zq], axis=-1).reshape(MLA_Q_LORA, -1)
    wqb = jnp.concatenate([jnp.zeros_like(wq[..., :MLA_NOPE]), _rot_cols(wq[..., MLA_NOPE:]), zq],
                          axis=-1).reshape(MLA_Q_LORA, -1)
    wkv = w_ukv.reshape(MLA_KV_LORA, MLA_HEADS, MLA_NOPE + MLA_V)
    wka = _pad_heads(wkv[..., :MLA_NOPE].reshape(MLA_KV_LORA, -1), MLA_HEADS, MLA_NOPE)
    wv = _pad_heads(wkv[..., MLA_NOPE:].reshape(MLA_KV_LORA, -1), MLA_HEADS, MLA_V)

    lane = jnp.arange(MLA_HEADS * HEAD_PAD) % HEAD_PAD
    src = jnp.arange(LANES)[:, None]
    in_rope = (lane >= MLA_NOPE) & (lane < MLA_NOPE + MLA_ROPE)
    pa = ((src == (lane - MLA_NOPE + _M_KR)[None, :]) & in_rope[None, :]).astype(BF16)
    pb = ((src == (lane - MLA_NOPE + _M_KROT)[None, :]) & in_rope[None, :]).astype(BF16)

    def gate_w(w, b, lane0):
        wpad = _pad_heads(w, GLA_HEADS, GLA_DK)
        full = jnp.zeros((LANES, wpad.shape[1]), F32).at[lane0:lane0 + GLA_GATE_RANK].set(wpad)
        return full.astype(BF16), _pad_heads(b.reshape(1, -1), GLA_HEADS, GLA_DK)

    gfw, gfb = gate_w(gate_fwd_w, gate_fwd_b, _M_GF)
    gbw, gbb = gate_w(gate_bwd_w, gate_bwd_b, _M_GB)

    half = MLA_ROPE // 2
    freqs = ROPE_THETA ** (-jnp.arange(half, dtype=F32) * 2.0 / MLA_ROPE)
    ang = jnp.arange(seq, dtype=F32)[:, None] * freqs[None, :]
    cos, sin = jnp.cos(ang), jnp.sin(ang)
    zpad = jnp.zeros((seq, HEAD_PAD - MLA_NOPE - MLA_ROPE), F32)
    ck = jnp.concatenate([jnp.ones((seq, MLA_NOPE), F32), cos, cos, zpad], axis=1)
    sk = jnp.concatenate([jnp.zeros((seq, MLA_NOPE), F32), sin, sin, zpad], axis=1)
    scale = (MLA_NOPE + MLA_ROPE) ** -0.5

    woa = w_o[:MLA_HEADS * MLA_V].reshape(MLA_HEADS, MLA_V, D_MODEL)
    woa = jnp.pad(woa, ((0, 0), (0, HEAD_PAD - MLA_V), (0, 0))).reshape(MLA_HEADS * HEAD_PAD, D_MODEL)

    return dict(
        g1=row(norm1_g), w1=w1.astype(BF16), gq=row(q_norm_g), gkv=row(kv_norm_g),
        wqa=wqa.astype(BF16), wqb=wqb.astype(BF16), wka=wka.astype(BF16), wv=wv.astype(BF16),
        pa=pa, pb=pb, gfw=gfw, gfb=gfb, gbw=gbw, gbb=gbb,
        cq=ck * scale, sq=sk * scale, ck=ck, sk=sk,
        gn=row(gla_norm_g), woa=woa.astype(BF16), wob=w_o[MLA_HEADS * MLA_V:].astype(BF16),
        g2=row(norm2_g), wqt=peer_wq.T.astype(BF16),
        subk=peer_subkeys.reshape(2 * PEER_HEADS, PEER_NKEYS, PEER_QDIM // 2).astype(BF16),
        u=peer_u.astype(BF16), vt=peer_v.T.astype(BF16), gfin=row(final_norm_g),
    )


def _tile(n, want):
    t = min(n, want)
    assert n % t == 0, (n, want)
    return t


def _trunk(x, wp):
    b, seq, d = x.shape
    x2 = x.reshape(b * seq, d)
    n = b * seq
    tm = _tile(seq, 512)
    q, k, v, gq, gk, gv, gf, gb, sg = _proj(x2, wp, seq, tm)
    attn = _attn(q, k, v, seq, _tile(seq, 256))
    ogla = _gla(gq, gk, gv, gf, gb, sg, wp["gn"], seq)
    h, xn = _mix(x2, attn, ogla, wp, tm)
    sel = _select(xn, wp, _tile(n, 256))
    y = _peer(xn, h, sel, wp, _tile(n, 512), 1024)
    return y.reshape(b, seq, d)


def kernel(x_prompt, x_sample, norm1_g, w_in, q_norm_g, w_uq, kv_norm_g, w_ukv, gate_fwd_w, gate_fwd_b,
           gate_bwd_w, gate_bwd_b, gla_norm_g, w_o, norm2_g, peer_wq, peer_subkeys, peer_u, peer_v,
           final_norm_g):
    assert norm1_g.shape[0] == 1, "single layer trunk"
    assert x_prompt.shape[1] == x_sample.shape[1]
    wp = _prep(x_prompt.shape[1], norm1_g[0], w_in[0], q_norm_g[0], w_uq[0], kv_norm_g[0], w_ukv[0],
               gate_fwd_w[0], gate_fwd_b[0], gate_bwd_w[0], gate_bwd_b[0], gla_norm_g[0], w_o[0],
               norm2_g[0], peer_wq[0], peer_subkeys[0], peer_u[0], peer_v[0], final_norm_g)
    return _trunk(x_prompt, wp), _trunk(x_sample, wp)
```

```python
import functools

import jax
import jax.numpy as jnp
from jax import lax
from jax.experimental import pallas as pl
from jax.experimental.pallas import tpu as pltpu

D_MODEL = 1024
MLA_HEADS = 8
MLA_Q_LORA = 256
MLA_KV_LORA = 128
MLA_NOPE = 64
MLA_ROPE = 32
MLA_V = 64
ROPE_THETA = 10000.0
GLA_HEADS = 4
GLA_DK = 64
GLA_DV = 128
GLA_GATE_RANK = 16
GLA_GATE_NORM = 16.0
GLA_CHUNK = 64
GLA_GROUP = 4
PEER_HEADS = 8
PEER_NKEYS = 128
PEER_NEXP = PEER_NKEYS * PEER_NKEYS
PEER_QDIM = 256
PEER_TOPK = 16
EPS = 1e-6

LANES = 128
HEAD_PAD = LANES
VMEM_LIMIT_BYTES = 56 * 1024 * 1024

_C_Q = 0
_C_KV = _C_Q + MLA_Q_LORA
_C_GQ = _C_KV + MLA_KV_LORA
_C_GK = _C_GQ + GLA_HEADS * HEAD_PAD
_C_GV = _C_GK + GLA_HEADS * HEAD_PAD
_C_GO = _C_GV + GLA_HEADS * GLA_DV
_C_MISC = _C_GO + GLA_HEADS * GLA_DV
_IN_COLS = _C_MISC + LANES
_M_KR = 0
_M_KROT = MLA_ROPE
_M_GF = 2 * MLA_ROPE
_M_GB = _M_GF + GLA_GATE_RANK

BF16 = jnp.bfloat16
F32 = jnp.float32
PACK = 16
ROW_WORD = jnp.uint32


def _dot(a, b):
    return jnp.dot(a, b, preferred_element_type=F32)


def _dot_nt(a, b):
    return lax.dot_general(a, b, (((1,), (1,)), ((), ())), preferred_element_type=F32)


def _dot_tn(a, b):
    return lax.dot_general(a, b, (((0,), (0,)), ((), ())), preferred_element_type=F32)


def _rms(x, g):
    return x * lax.rsqrt(jnp.mean(x * x, axis=-1, keepdims=True) + EPS) * g


def _log_sigmoid(x):
    return jnp.minimum(x, 0.0) - jnp.log(1.0 + jnp.exp(-jnp.abs(x)))


def _gelu_tanh(x):
    k = -2.0 * 0.7978845608028654 * 1.4426950408889634
    return x * (1.0 / (1.0 + jnp.exp2(x * (k + (k * 0.044715) * (x * x)))))


def _params(*sem):
    return pltpu.CompilerParams(dimension_semantics=sem, vmem_limit_bytes=VMEM_LIMIT_BYTES)


def _full(shape):
    nd = len(shape)
    return pl.BlockSpec(shape, lambda *_: (0,) * nd)


def _proj_kernel(x_ref, g1_ref, w1_ref, gq_ref, gkv_ref, wqa_ref, wqb_ref, wka_ref, wv_ref,
                 pa_ref, pb_ref, gfw_ref, gfb_ref, gbw_ref, gbb_ref,
                 cq_ref, sq_ref, ck_ref, sk_ref,
                 q_ref, k_ref, v_ref, gq_o, gk_o, gv_o, gf_o, gb_o, sg_o):
    x = x_ref[...]
    n1 = _rms(x, g1_ref[...]).astype(BF16)
    p = _dot(n1, w1_ref[...])
    cqn = _rms(p[:, _C_Q:_C_Q + MLA_Q_LORA], gq_ref[...]).astype(BF16)
    ckvn = _rms(p[:, _C_KV:_C_KV + MLA_KV_LORA], gkv_ref[...]).astype(BF16)
    misc = p[:, _C_MISC:_C_MISC + LANES].astype(BF16)

    qa = _dot(cqn, wqa_ref[...])
    qb = _dot(cqn, wqb_ref[...])
    ka = _dot(ckvn, wka_ref[...]) + _dot(misc, pa_ref[...])
    kb = _dot(misc, pb_ref[...])
    cq, sq, ck, sk = cq_ref[...], sq_ref[...], ck_ref[...], sk_ref[...]
    for h in range(MLA_HEADS):
        sl = slice(h * HEAD_PAD, (h + 1) * HEAD_PAD)
        q_ref[:, sl] = (qa[:, sl] * cq + qb[:, sl] * sq).astype(BF16)
        k_ref[:, sl] = (ka[:, sl] * ck + kb[:, sl] * sk).astype(BF16)
    v_ref[...] = _dot(ckvn, wv_ref[...]).astype(BF16)

    w = GLA_HEADS * HEAD_PAD
    gq_o[...] = p[:, _C_GQ:_C_GQ + w].astype(BF16)
    gk_o[...] = p[:, _C_GK:_C_GK + w].astype(BF16)
    gv_o[...] = p[:, _C_GV:_C_GV + GLA_HEADS * GLA_DV].astype(BF16)
    go = p[:, _C_GO:_C_GO + GLA_HEADS * GLA_DV]
    sg_o[...] = (go * (1.0 / (1.0 + jnp.exp(-go)))).astype(BF16)
    gf_o[...] = _log_sigmoid(_dot(misc, gfw_ref[...]) + gfb_ref[...]) * (1.0 / GLA_GATE_NORM)
    gb_o[...] = _log_sigmoid(_dot(misc, gbw_ref[...]) + gbb_ref[...]) * (1.0 / GLA_GATE_NORM)


def _proj(x2, wp, seq, tm):
    n = x2.shape[0]
    tiles_per_seq = seq // tm
    tok = lambda w: pl.BlockSpec((tm, w), lambda i: (i, 0))
    tab = pl.BlockSpec((tm, HEAD_PAD), lambda i: (i % tiles_per_seq, 0))
    weights = [wp["g1"], wp["w1"], wp["gq"], wp["gkv"], wp["wqa"], wp["wqb"], wp["wka"], wp["wv"],
               wp["pa"], wp["pb"], wp["gfw"], wp["gfb"], wp["gbw"], wp["gbb"]]
    tables = [wp["cq"], wp["sq"], wp["ck"], wp["sk"]]
    hp = MLA_HEADS * HEAD_PAD
    gw = GLA_HEADS * HEAD_PAD
    gv = GLA_HEADS * GLA_DV
    out_w = [(hp, BF16), (hp, BF16), (hp, BF16), (gw, BF16), (gw, BF16), (gv, BF16),
             (gw, F32), (gw, F32), (gv, BF16)]
    return pl.pallas_call(
        _proj_kernel,
        grid=(n // tm,),
        in_specs=[tok(D_MODEL)] + [_full(w.shape) for w in weights] + [tab] * 4,
        out_specs=[tok(w) for w, _ in out_w],
        out_shape=[jax.ShapeDtypeStruct((n, w), dt) for w, dt in out_w],
        compiler_params=_params("parallel"),
        name="proj",
    )(x2, *weights, *tables)


def _attn_kernel(q_ref, k_ref, v_ref, o_ref):
    s = _dot_nt(q_ref[...], k_ref[...])
    m = jnp.max(s, axis=-1, keepdims=True)
    p = jnp.exp(s - m)
    l = jnp.sum(p, axis=-1, keepdims=True)
    o = _dot(p.astype(BF16), v_ref[...])
    o_ref[...] = (o * (1.0 / l)).astype(BF16)


def _attn(q, k, v, seq, tq):
    n = q.shape[0]
    nb = n // seq
    nq = seq // tq
    return pl.pallas_call(
        _attn_kernel,
        grid=(nb, MLA_HEADS, nq),
        in_specs=[pl.BlockSpec((tq, HEAD_PAD), lambda b, h, i: (b * nq + i, h)),
                  pl.BlockSpec((seq, HEAD_PAD), lambda b, h, i: (b, h)),
                  pl.BlockSpec((seq, HEAD_PAD), lambda b, h, i: (b, h))],
        out_specs=pl.BlockSpec((tq, HEAD_PAD), lambda b, h, i: (b * nq + i, h)),
        out_shape=jax.ShapeDtypeStruct((n, MLA_HEADS * HEAD_PAD), BF16),
        compiler_params=_params("parallel", "parallel", "arbitrary"),
        name="attn",
    )(q, k, v)


def _gla_kernel(q_ref, k_ref, v_ref, gf_ref, gb_ref, sg_ref, gn_ref, o_ref, of_ref, ob_ref, *, seq):
    c = GLA_CHUNK
    span = c * GLA_GROUP
    ntrip = seq // span
    row = lax.broadcasted_iota(jnp.int32, (span, span), 0)
    col = lax.broadcasted_iota(jnp.int32, (span, span), 1)
    same_chunk = (row // c) == (col // c)
    causal_f = same_chunk & (row >= col)
    causal_b = same_chunk & (row <= col)

    def stage_local(rows, g_ref, causal, edge):
        q = q_ref[rows, :].astype(F32)
        k = k_ref[rows, :].astype(F32)
        v = v_ref[rows, :]
        g = g_ref[rows, :]
        tri = jnp.where(causal, 1.0, 0.0).astype(BF16)
        hi = g.astype(BF16)
        lo = (g - hi.astype(F32)).astype(BF16)
        b = _dot(tri, hi) + _dot(tri, lo)
        edges = [b[j * c + edge:j * c + edge + 1, :] for j in range(GLA_GROUP)]
        b_edge = jnp.concatenate([jnp.broadcast_to(e, (c, HEAD_PAD)) for e in edges], axis=0)
        q_dec = (q * jnp.exp(b)).astype(BF16)
        k_inc = (k * jnp.exp(-b)).astype(BF16)
        k_dec = (k * jnp.exp(b_edge - b)).astype(BF16)
        att = jnp.where(causal, _dot_nt(q_dec, k_inc), 0.0).astype(BF16)
        o_intra = _dot(att, v)
        kv_t = [_dot_tn(v[j * c:(j + 1) * c, :], k_dec[j * c:(j + 1) * c, :]) for j in range(GLA_GROUP)]
        decay = [jnp.exp(e) for e in edges]
        return q_dec, o_intra, kv_t, decay

    def stage_state(order, local, state_t):
        q_dec, o_intra, kv_t, decay = local
        outs = [None] * GLA_GROUP
        for j in order:
            rows = slice(j * c, (j + 1) * c)
            outs[j] = o_intra[rows, :] + _dot_nt(q_dec[rows, :], state_t.astype(BF16))
            state_t = state_t * decay[j] + kv_t[j]
        return jnp.concatenate(outs, axis=0), state_t

    def body(i, carry):
        sf, sb = carry
        rf = pl.ds(pl.multiple_of(i * span, span), span)
        rb = pl.ds(pl.multiple_of((ntrip - 1 - i) * span, span), span)
        local_f = stage_local(rf, gf_ref, causal_f, c - 1)
        local_b = stage_local(rb, gb_ref, causal_b, 0)
        o_f, sf = stage_state(range(GLA_GROUP), local_f, sf)
        o_b, sb = stage_state(reversed(range(GLA_GROUP)), local_b, sb)
        of_ref[rf, :] = o_f
        ob_ref[rb, :] = o_b
        return sf, sb

    zero = jnp.zeros((GLA_DV, HEAD_PAD), F32)
    lax.fori_loop(0, ntrip, body, (zero, zero))
    o = of_ref[...] + ob_ref[...]
    o_ref[...] = (_rms(o, gn_ref[...]) * sg_ref[...].astype(F32)).astype(BF16)


def _gla(gq, gk, gv, gf, gb, sg, gn, seq):
    n = gq.shape[0]
    nb = n // seq
    blk = lambda w: pl.BlockSpec((seq, w), lambda b, h: (b, h))
    return pl.pallas_call(
        functools.partial(_gla_kernel, seq=seq),
        grid=(nb, GLA_HEADS),
        in_specs=[blk(HEAD_PAD), blk(HEAD_PAD), blk(GLA_DV), blk(HEAD_PAD), blk(HEAD_PAD), blk(GLA_DV),
                  _full(gn.shape)],
        out_specs=blk(GLA_DV),
        out_shape=jax.ShapeDtypeStruct((n, GLA_HEADS * GLA_DV), BF16),
        scratch_shapes=[pltpu.VMEM((seq, GLA_DV), F32), pltpu.VMEM((seq, GLA_DV), F32)],
        compiler_params=_params("parallel", "parallel"),
        name="gla",
    )(gq, gk, gv, gf, gb, sg, gn)


def _mix_kernel(x_ref, a_ref, g_ref, woa_ref, wob_ref, g2_ref, h_ref, xn_ref):
    h = x_ref[...] + _dot(a_ref[...], woa_ref[...]) + _dot(g_ref[...], wob_ref[...])
    h_ref[...] = h
    xn_ref[...] = _rms(h, g2_ref[...]).astype(BF16)


def _mix(x2, attn, ogla, wp, tm):
    n = x2.shape[0]
    tok = lambda w: pl.BlockSpec((tm, w), lambda i: (i, 0))
    return pl.pallas_call(
        _mix_kernel,
        grid=(n // tm,),
        in_specs=[tok(D_MODEL), tok(attn.shape[1]), tok(ogla.shape[1]),
                  _full(wp["woa"].shape), _full(wp["wob"].shape), _full(wp["g2"].shape)],
        out_specs=[tok(D_MODEL), tok(D_MODEL)],
        out_shape=[jax.ShapeDtypeStruct((n, D_MODEL), F32), jax.ShapeDtypeStruct((n, D_MODEL), BF16)],
        compiler_params=_params("parallel"),
        name="mix",
    )(x2, attn, ogla, wp["woa"], wp["wob"], wp["g2"])


def _top_values(s, count):
    cur = jnp.full((1, s.shape[1]), jnp.inf, F32)
    vals = []
    for _ in range(count):
        cur = jnp.max(jnp.where(s < cur, s, -jnp.inf), axis=0, keepdims=True)
        vals.append(cur)
    return vals


def _dup_bf16(x):
    w = pltpu.bitcast(x.astype(BF16).astype(F32), ROW_WORD)
    return w | (w >> 16)


def _packed_row(word):
    return pltpu.bitcast(jnp.broadcast_to(word, (PACK // 2, word.shape[1])), BF16)


def _select_kernel(xn_ref, wqt_ref, sk_ref, r2_ref, e2_ref, e1_ref, cnt_ref):
    qt = _dot_nt(wqt_ref[...], xn_ref[...])
    half = PEER_QDIM // 2
    for h in range(PEER_HEADS):
        s, tops = [], []
        for p in range(2):
            r0 = (2 * h + p) * half
            sp = _dot(sk_ref[2 * h + p], qt[r0:r0 + half, :].astype(BF16))
            s.append(sp)
            tops.append(_top_values(sp, PEER_TOPK))
        top1 = jnp.concatenate(tops[0], axis=0)
        top2 = jnp.concatenate(tops[1], axis=0)
        oct_ = PEER_TOPK // 2
        sub = lax.broadcasted_iota(jnp.int32, (oct_, top2.shape[1]), 0)
        rows = [tops[0][0] + top2]
        for a in range(1, oct_):
            rows.append(jnp.where(sub < PEER_TOPK // (a + 1), tops[0][a] + top2[:oct_, :], -jnp.inf))
        tail = top1[oct_:, :] + tops[1][0]
        cand = jnp.concatenate(rows + [tail], axis=0)
        thr = _top_values(cand, PEER_TOPK)[-1]
        best = tops[0][0] + tops[1][0]
        z = jnp.sum(jnp.where(cand >= thr, jnp.exp(cand - best), 0.0), axis=0, keepdims=True)
        counts = [jnp.sum(jnp.where(r >= thr, 1.0, 0.0), axis=0, keepdims=True) for r in rows]
        tail_hit = jnp.where(tail >= thr, 1.0, 0.0)
        counts += [tail_hit[a:a + 1, :] for a in range(PEER_TOPK - oct_)]
        cnt = jnp.zeros_like(s[0])
        for a in reversed(range(PEER_TOPK)):
            cnt = jnp.where(s[0] >= tops[0][a], counts[a], cnt)
        rank2 = jnp.zeros_like(s[1])
        for a in range(PEER_TOPK):
            rank2 = jnp.where(s[1] < tops[1][a], a + 1.0, rank2)
        r2_ref[h] = rank2.astype(BF16)
        e2_ref[h] = (jnp.exp(s[1] - tops[1][0]) * (1.0 / z)).astype(BF16)
        e1_ref[h] = _dup_bf16(jnp.exp(s[0] - tops[0][0]))
        cnt_ref[h] = _dup_bf16(cnt)


def _select(xn, wp, tb):
    n = xn.shape[0]
    blk = pl.BlockSpec((PEER_HEADS, PEER_NKEYS, tb), lambda i: (0, 0, i))
    shp = lambda dt: jax.ShapeDtypeStruct((PEER_HEADS, PEER_NKEYS, n), dt)
    return pl.pallas_call(
        _select_kernel,
        grid=(n // tb,),
        in_specs=[pl.BlockSpec((tb, D_MODEL), lambda i: (i, 0)),
                  _full(wp["wqt"].shape), _full(wp["subk"].shape)],
        out_specs=[blk, blk, blk, blk],
        out_shape=[shp(BF16), shp(BF16), shp(ROW_WORD), shp(ROW_WORD)],
        compiler_params=_params("parallel"),
        name="peer_select",
    )(xn, wp["wqt"], wp["subk"])


def _peer_kernel(xn_ref, h_ref, r2_ref, e2_ref, e1_ref, cnt_ref, u_ref, vt_ref, gfin_ref,
                 o_ref, acc_ref, g_ref, *, rows_per_chunk):
    c = pl.program_id(1)

    @pl.when(c == 0)
    def _():
        acc_ref[...] = jnp.zeros_like(acc_ref)

    act = _gelu_tanh(_dot_nt(u_ref[...], xn_ref[...]))

    for j in range(rows_per_chunk):
        e1 = [_packed_row(e1_ref[h, j:j + 1, :]) for h in range(PEER_HEADS)]
        cnt = [_packed_row(cnt_ref[h, j:j + 1, :]) for h in range(PEER_HEADS)]
        for m in range(PEER_NKEYS // PACK):
            keys = slice(m * PACK, (m + 1) * PACK)
            w = None
            for h in range(PEER_HEADS):
                term = jnp.where(r2_ref[h, keys, :] < cnt[h], e2_ref[h, keys, :], 0.0) * e1[h]
                w = term if w is None else w + term
            rows = slice(j * PEER_NKEYS + m * PACK, j * PEER_NKEYS + (m + 1) * PACK)
            g_ref[rows, :] = act[rows, :].astype(BF16) * w
    acc_ref[...] += _dot(vt_ref[...], g_ref[...])

    @pl.when(c == pl.num_programs(1) - 1)
    def _():
        y = h_ref[...] + acc_ref[...].T
        o_ref[...] = _rms(y, gfin_ref[...])


def _peer(xn, h, sel, wp, tb, ec):
    n = xn.shape[0]
    r2, e2, e1, cnt = sel
    rpc = ec // PEER_NKEYS
    tok = lambda: pl.BlockSpec((tb, D_MODEL), lambda i, c: (i, 0))
    part = pl.BlockSpec((PEER_HEADS, rpc, tb), lambda i, c: (0, c, i))
    whole = pl.BlockSpec((PEER_HEADS, PEER_NKEYS, tb), lambda i, c: (0, 0, i))
    return pl.pallas_call(
        functools.partial(_peer_kernel, rows_per_chunk=rpc),
        grid=(n // tb, PEER_NEXP // ec),
        in_specs=[tok(), tok(), whole, whole, part, part,
                  pl.BlockSpec((ec, D_MODEL), lambda i, c: (c, 0)),
                  pl.BlockSpec((D_MODEL, ec), lambda i, c: (0, c)),
                  _full(wp["gfin"].shape)],
        out_specs=tok(),
        out_shape=jax.ShapeDtypeStruct((n, D_MODEL), F32),
        scratch_shapes=[pltpu.VMEM((D_MODEL, tb), F32), pltpu.VMEM((ec, tb), BF16)],
        compiler_params=_params("parallel", "arbitrary"),
        name="peer_dense",
    )(xn, h, r2, e2, e1, cnt, wp["u"], wp["vt"], wp["gfin"])


def _pad_heads(w, heads, width):
    r = w.shape[0]
    w = w.reshape(r, heads, width)
    return jnp.pad(w, ((0, 0), (0, 0), (0, HEAD_PAD - width))).reshape(r, heads * HEAD_PAD)


def _rot_cols(w):
    half = MLA_ROPE // 2
    return jnp.concatenate([-w[..., half:], w[..., :half]], axis=-1)


def _prep(seq, norm1_g, w_in, q_norm_g, w_uq, kv_norm_g, w_ukv, gate_fwd_w, gate_fwd_b,
          gate_bwd_w, gate_bwd_b, gla_norm_g, w_o, norm2_g, peer_wq, peer_subkeys, peer_u, peer_v,
          final_norm_g):
    row = lambda g: g.reshape(1, -1).astype(F32)
    o = 0
    segs = {}
    for name, width in (("cq", MLA_Q_LORA), ("ckv", MLA_KV_LORA), ("kr", MLA_ROPE),
                        ("gq", GLA_HEADS * GLA_DK), ("gk", GLA_HEADS * GLA_DK), ("gv", GLA_HEADS * GLA_DV),
                        ("lf", GLA_GATE_RANK), ("lb", GLA_GATE_RANK), ("go", GLA_HEADS * GLA_DV)):
        segs[name] = w_in[:, o:o + width]
        o += width
    misc = jnp.concatenate([segs["kr"], _rot_cols(segs["kr"]), segs["lf"], segs["lb"],
                            jnp.zeros((D_MODEL, LANES - _M_GB - GLA_GATE_RANK), F32)], axis=1)
    w1 = jnp.concatenate([segs["cq"], segs["ckv"],
                          _pad_heads(segs["gq"] * (GLA_DK ** -0.5), GLA_HEADS, GLA_DK),
                          _pad_heads(segs["gk"], GLA_HEADS, GLA_DK),
                          segs["gv"], segs["go"], misc], axis=1)

    wq = w_uq.reshape(MLA_Q_LORA, MLA_HEADS, MLA_NOPE + MLA_ROPE)
    zq = jnp.zeros((MLA_Q_LORA, MLA_HEADS, HEAD_PAD - MLA_NOPE - MLA_ROPE), F32)
    wqa = jnp.concatenate([wq, zq], axis=-1).reshape(MLA_Q_LORA, -1)
    wqb = jnp.concatenate([jnp.zeros_like(wq[..., :MLA_NOPE]), _rot_cols(wq[..., MLA_NOPE:]), zq],
                          axis=-1).reshape(MLA_Q_LORA, -1)
    wkv = w_ukv.reshape(MLA_KV_LORA, MLA_HEADS, MLA_NOPE + MLA_V)
    wka = _pad_heads(wkv[..., :MLA_NOPE].reshape(MLA_KV_LORA, -1), MLA_HEADS, MLA_NOPE)
    wv = _pad_heads(wkv[..., MLA_NOPE:].reshape(MLA_KV_LORA, -1), MLA_HEADS, MLA_V)

    lane = jnp.arange(MLA_HEADS * HEAD_PAD) % HEAD_PAD
    src = jnp.arange(LANES)[:, None]
    in_rope = (lane >= MLA_NOPE) & (lane < MLA_NOPE + MLA_ROPE)
    pa = ((src == (lane - MLA_NOPE + _M_KR)[None, :]) & in_rope[None, :]).astype(BF16)
    pb = ((src == (lane - MLA_NOPE + _M_KROT)[None, :]) & in_rope[None, :]).astype(BF16)

    def gate_w(w, b, lane0):
        wpad = _pad_heads(w, GLA_HEADS, GLA_DK)
        full = jnp.zeros((LANES, wpad.shape[1]), F32).at[lane0:lane0 + GLA_GATE_RANK].set(wpad)
        return full.astype(BF16), _pad_heads(b.reshape(1, -1), GLA_HEADS, GLA_DK)

    gfw, gfb = gate_w(gate_fwd_w, gate_fwd_b, _M_GF)
    gbw, gbb = gate_w(gate_bwd_w, gate_bwd_b, _M_GB)

    half = MLA_ROPE // 2
    freqs = ROPE_THETA ** (-jnp.arange(half, dtype=F32) * 2.0 / MLA_ROPE)
    ang = jnp.arange(seq, dtype=F32)[:, None] * freqs[None, :]
    cos, sin = jnp.cos(ang), jnp.sin(ang)
    zpad = jnp.zeros((seq, HEAD_PAD - MLA_NOPE - MLA_ROPE), F32)
    ck = jnp.concatenate([jnp.ones((seq, MLA_NOPE), F32), cos, cos, zpad], axis=1)
    sk = jnp.concatenate([jnp.zeros((seq, MLA_NOPE), F32), sin, sin, zpad], axis=1)
    scale = (MLA_NOPE + MLA_ROPE) ** -0.5

    woa = w_o[:MLA_HEADS * MLA_V].reshape(MLA_HEADS, MLA_V, D_MODEL)
    woa = jnp.pad(woa, ((0, 0), (0, HEAD_PAD - MLA_V), (0, 0))).reshape(MLA_HEADS * HEAD_PAD, D_MODEL)

    return dict(
        g1=row(norm1_g), w1=w1.astype(BF16), gq=row(q_norm_g), gkv=row(kv_norm_g),
        wqa=wqa.astype(BF16), wqb=wqb.astype(BF16), wka=wka.astype(BF16), wv=wv.astype(BF16),
        pa=pa, pb=pb, gfw=gfw, gfb=gfb, gbw=gbw, gbb=gbb,
        cq=ck * scale, sq=sk * scale, ck=ck, sk=sk,
        gn=row(gla_norm_g), woa=woa.astype(BF16), wob=w_o[MLA_HEADS * MLA_V:].astype(BF16),
        g2=row(norm2_g), wqt=peer_wq.T.astype(BF16),
        subk=peer_subkeys.reshape(2 * PEER_HEADS, PEER_NKEYS, PEER_QDIM // 2).astype(BF16),
        u=peer_u.astype(BF16), vt=peer_v.T.astype(BF16), gfin=row(final_norm_g),
    )


def _tile(n, want):
    t = min(n, want)
    assert n % t == 0, (n, want)
    return t


def _trunk(x, wp):
    b, seq, d = x.shape
    x2 = x.reshape(b * seq, d)
    n = b * seq
    tm = _tile(seq, 512)
    q, k, v, gq, gk, gv, gf, gb, sg = _proj(x2, wp, seq, tm)
    attn = _attn(q, k, v, seq, _tile(seq, 256))
    ogla = _gla(gq, gk, gv, gf, gb, sg, wp["gn"], seq)
    h, xn = _mix(x2, attn, ogla, wp, tm)
    sel = _select(xn, wp, _tile(n, 256))
    y = _peer(xn, h, sel, wp, _tile(n, 512), 1024)
    return y.reshape(b, seq, d)


def kernel(x_prompt, x_sample, norm1_g, w_in, q_norm_g, w_uq, kv_norm_g, w_ukv, gate_fwd_w, gate_fwd_b,
           gate_bwd_w, gate_bwd_b, gla_norm_g, w_o, norm2_g, peer_wq, peer_subkeys, peer_u, peer_v,
           final_norm_g):
    assert norm1_g.shape[0] == 1, "single layer trunk"
    assert x_prompt.shape[1] == x_sample.shape[1]
    wp = _prep(x_prompt.shape[1], norm1_g[0], w_in[0], q_norm_g[0], w_uq[0], kv_norm_g[0], w_ukv[0],
               gate_fwd_w[0], gate_fwd_b[0], gate_bwd_w[0], gate_bwd_b[0], gla_norm_g[0], w_o[0],
               norm2_g[0], peer_wq[0], peer_subkeys[0], peer_u[0], peer_v[0], final_norm_g)
    return _trunk(x_prompt, wp), _trunk(x_sample, wp)
```

```python
import functools

import jax
import jax.numpy as jnp
from jax import lax
from jax.experimental import pallas as pl
from jax.experimental.pallas import tpu as pltpu

D_MODEL = 1024
MLA_HEADS = 8
MLA_Q_LORA = 256
MLA_KV_LORA = 128
MLA_NOPE = 64
MLA_ROPE = 32
MLA_V = 64
ROPE_THETA = 10000.0
GLA_HEADS = 4
GLA_DK = 64
GLA_DV = 128
GLA_GATE_RANK = 16
GLA_GATE_NORM = 16.0
GLA_CHUNK = 64
GLA_GROUP = 4
PEER_HEADS = 8
PEER_NKEYS = 128
PEER_NEXP = PEER_NKEYS * PEER_NKEYS
PEER_QDIM = 256
PEER_TOPK = 16
EPS = 1e-6

LANES = 128
HEAD_PAD = LANES
VMEM_LIMIT_BYTES = 56 * 1024 * 1024

_C_Q = 0
_C_KV = _C_Q + MLA_Q_LORA
_C_GQ = _C_KV + MLA_KV_LORA
_C_GK = _C_GQ + GLA_HEADS * HEAD_PAD
_C_GV = _C_GK + GLA_HEADS * HEAD_PAD
_C_GO = _C_GV + GLA_HEADS * GLA_DV
_C_MISC = _C_GO + GLA_HEADS * GLA_DV
_IN_COLS = _C_MISC + LANES
_M_KR = 0
_M_KROT = MLA_ROPE
_M_GF = 2 * MLA_ROPE
_M_GB = _M_GF + GLA_GATE_RANK

BF16 = jnp.bfloat16
F32 = jnp.float32
PACK = 16
ROW_WORD = jnp.uint32


def _dot(a, b):
    return jnp.dot(a, b, preferred_element_type=F32)


def _dot_nt(a, b):
    return lax.dot_general(a, b, (((1,), (1,)), ((), ())), preferred_element_type=F32)


def _dot_tn(a, b):
    return lax.dot_general(a, b, (((0,), (0,)), ((), ())), preferred_element_type=F32)


def _rms(x, g):
    return x * lax.rsqrt(jnp.mean(x * x, axis=-1, keepdims=True) + EPS) * g


def _log_sigmoid(x):
    return jnp.minimum(x, 0.0) - jnp.log(1.0 + jnp.exp(-jnp.abs(x)))


def _gelu_tanh(x):
    k = -2.0 * 0.7978845608028654 * 1.4426950408889634
    return x * (1.0 / (1.0 + jnp.exp2(x * (k + (k * 0.044715) * (x * x)))))


def _params(*sem):
    return pltpu.CompilerParams(dimension_semantics=sem, vmem_limit_bytes=VMEM_LIMIT_BYTES)


def _full(shape):
    nd = len(shape)
    return pl.BlockSpec(shape, lambda *_: (0,) * nd)


def _proj_kernel(x_ref, g1_ref, w1_ref, gq_ref, gkv_ref, wqa_ref, wqb_ref, wka_ref, wv_ref,
                 pa_ref, pb_ref, gfw_ref, gfb_ref, gbw_ref, gbb_ref,
                 cq_ref, sq_ref, ck_ref, sk_ref,
                 q_ref, k_ref, v_ref, gq_o, gk_o, gv_o, gf_o, gb_o, sg_o):
    x = x_ref[...]
    n1 = _rms(x, g1_ref[...]).astype(BF16)
    p = _dot(n1, w1_ref[...])
    cqn = _rms(p[:, _C_Q:_C_Q + MLA_Q_LORA], gq_ref[...]).astype(BF16)
    ckvn = _rms(p[:, _C_KV:_C_KV + MLA_KV_LORA], gkv_ref[...]).astype(BF16)
    misc = p[:, _C_MISC:_C_MISC + LANES].astype(BF16)

    qa = _dot(cqn, wqa_ref[...])
    qb = _dot(cqn, wqb_ref[...])
    ka = _dot(ckvn, wka_ref[...]) + _dot(misc, pa_ref[...])
    kb = _dot(misc, pb_ref[...])
    cq, sq, ck, sk = cq_ref[...], sq_ref[...], ck_ref[...], sk_ref[...]
    for h in range(MLA_HEADS):
        sl = slice(h * HEAD_PAD, (h + 1) * HEAD_PAD)
        q_ref[:, sl] = (qa[:, sl] * cq + qb[:, sl] * sq).astype(BF16)
        k_ref[:, sl] = (ka[:, sl] * ck + kb[:, sl] * sk).astype(BF16)
    v_ref[...] = _dot(ckvn, wv_ref[...]).astype(BF16)

    w = GLA_HEADS * HEAD_PAD
    gq_o[...] = p[:, _C_GQ:_C_GQ + w].astype(BF16)
    gk_o[...] = p[:, _C_GK:_C_GK + w].astype(BF16)
    gv_o[...] = p[:, _C_GV:_C_GV + GLA_HEADS * GLA_DV].astype(BF16)
    go = p[:, _C_GO:_C_GO + GLA_HEADS * GLA_DV]
    sg_o[...] = (go * (1.0 / (1.0 + jnp.exp(-go)))).astype(BF16)
    gf_o[...] = _log_sigmoid(_dot(misc, gfw_ref[...]) + gfb_ref[...]) * (1.0 / GLA_GATE_NORM)
    gb_o[...] = _log_sigmoid(_dot(misc, gbw_ref[...]) + gbb_ref[...]) * (1.0 / GLA_GATE_NORM)


def _proj(x2, wp, seq, tm):
    n = x2.shape[0]
    tiles_per_seq = seq // tm
    tok = lambda w: pl.BlockSpec((tm, w), lambda i: (i, 0))
    tab = pl.BlockSpec((tm, HEAD_PAD), lambda i: (i % tiles_per_seq, 0))
    weights = [wp["g1"], wp["w1"], wp["gq"], wp["gkv"], wp["wqa"], wp["wqb"], wp["wka"], wp["wv"],
               wp["pa"], wp["pb"], wp["gfw"], wp["gfb"], wp["gbw"], wp["gbb"]]
    tables = [wp["cq"], wp["sq"], wp["ck"], wp["sk"]]
    hp = MLA_HEADS * HEAD_PAD
    gw = GLA_HEADS * HEAD_PAD
    gv = GLA_HEADS * GLA_DV
    out_w = [(hp, BF16), (hp, BF16), (hp, BF16), (gw, BF16), (gw, BF16), (gv, BF16),
             (gw, F32), (gw, F32), (gv, BF16)]
    return pl.pallas_call(
        _proj_kernel,
        grid=(n // tm,),
        in_specs=[tok(D_MODEL)] + [_full(w.shape) for w in weights] + [tab] * 4,
        out_specs=[tok(w) for w, _ in out_w],
        out_shape=[jax.ShapeDtypeStruct((n, w), dt) for w, dt in out_w],
        compiler_params=_params("parallel"),
        name="proj",
    )(x2, *weights, *tables)


ATTN_SUB = 256


def _attn_kernel(q_ref, k_ref, v_ref, o_ref):
    k = k_ref[...]
    v = v_ref[...]
    subs = [slice(i * ATTN_SUB, (i + 1) * ATTN_SUB) for i in range(q_ref.shape[0] // ATTN_SUB)]
    scores = [_dot_nt(q_ref[r, :], k) for r in subs]
    for r, s in zip(subs, scores):
        p = jnp.exp2(s - jnp.max(s, axis=-1, keepdims=True))
        l = jnp.sum(p, axis=-1, keepdims=True)
        o_ref[r, :] = (_dot(p.astype(BF16), v) * (1.0 / l)).astype(BF16)


def _attn(q, k, v, seq, tq):
    n = q.shape[0]
    nb = n // seq
    nq = seq // tq
    return pl.pallas_call(
        _attn_kernel,
        grid=(nb, MLA_HEADS, nq),
        in_specs=[pl.BlockSpec((tq, HEAD_PAD), lambda b, h, i: (b * nq + i, h)),
                  pl.BlockSpec((seq, HEAD_PAD), lambda b, h, i: (b, h)),
                  pl.BlockSpec((seq, HEAD_PAD), lambda b, h, i: (b, h))],
        out_specs=pl.BlockSpec((tq, HEAD_PAD), lambda b, h, i: (b * nq + i, h)),
        out_shape=jax.ShapeDtypeStruct((n, MLA_HEADS * HEAD_PAD), BF16),
        compiler_params=_params("parallel", "parallel", "arbitrary"),
        name="attn",
    )(q, k, v)


def _gla_kernel(q_ref, k_ref, v_ref, gf_ref, gb_ref, sg_ref, gn_ref, o_ref, of_ref, ob_ref, *, seq):
    c = GLA_CHUNK
    span = c * GLA_GROUP
    ntrip = seq // span
    row = lax.broadcasted_iota(jnp.int32, (span, span), 0)
    col = lax.broadcasted_iota(jnp.int32, (span, span), 1)
    same_chunk = (row // c) == (col // c)
    causal_f = same_chunk & (row >= col)
    causal_b = same_chunk & (row <= col)

    def stage_local(rows, g_ref, causal, edge):
        q = q_ref[rows, :].astype(F32)
        k = k_ref[rows, :].astype(F32)
        v = v_ref[rows, :]
        g = g_ref[rows, :]
        tri = jnp.where(causal, 1.0, 0.0).astype(BF16)
        hi = g.astype(BF16)
        lo = (g - hi.astype(F32)).astype(BF16)
        b = _dot(tri, hi) + _dot(tri, lo)
        edges = [b[j * c + edge:j * c + edge + 1, :] for j in range(GLA_GROUP)]
        b_edge = jnp.concatenate([jnp.broadcast_to(e, (c, HEAD_PAD)) for e in edges], axis=0)
        q_dec = (q * jnp.exp(b)).astype(BF16)
        k_inc = (k * jnp.exp(-b)).astype(BF16)
        k_dec = (k * jnp.exp(b_edge - b)).astype(BF16)
        att = jnp.where(causal, _dot_nt(q_dec, k_inc), 0.0).astype(BF16)
        o_intra = _dot(att, v)
        kv_t = [_dot_tn(v[j * c:(j + 1) * c, :], k_dec[j * c:(j + 1) * c, :]) for j in range(GLA_GROUP)]
        decay = [jnp.exp(e) for e in edges]
        return q_dec, o_intra, kv_t, decay

    def stage_state(order, local, state_t):
        q_dec, o_intra, kv_t, decay = local
        outs = [None] * GLA_GROUP
        for j in order:
            rows = slice(j * c, (j + 1) * c)
            outs[j] = o_intra[rows, :] + _dot_nt(q_dec[rows, :], state_t.astype(BF16))
            state_t = state_t * decay[j] + kv_t[j]
        return jnp.concatenate(outs, axis=0), state_t

    def body(i, carry):
        sf, sb = carry
        rf = pl.ds(pl.multiple_of(i * span, span), span)
        rb = pl.ds(pl.multiple_of((ntrip - 1 - i) * span, span), span)
        local_f = stage_local(rf, gf_ref, causal_f, c - 1)
        local_b = stage_local(rb, gb_ref, causal_b, 0)
        o_f, sf = stage_state(range(GLA_GROUP), local_f, sf)
        o_b, sb = stage_state(reversed(range(GLA_GROUP)), local_b, sb)
        of_ref[rf, :] = o_f
        ob_ref[rb, :] = o_b
        return sf, sb

    zero = jnp.zeros((GLA_DV, HEAD_PAD), F32)
    lax.fori_loop(0, ntrip, body, (zero, zero))
    o = of_ref[...] + ob_ref[...]
    o_ref[...] = (_rms(o, gn_ref[...]) * sg_ref[...].astype(F32)).astype(BF16)


def _gla(gq, gk, gv, gf, gb, sg, gn, seq):
    n = gq.shape[0]
    nb = n // seq
    blk = lambda w: pl.BlockSpec((seq, w), lambda b, h: (b, h))
    return pl.pallas_call(
        functools.partial(_gla_kernel, seq=seq),
        grid=(nb, GLA_HEADS),
        in_specs=[blk(HEAD_PAD), blk(HEAD_PAD), blk(GLA_DV), blk(HEAD_PAD), blk(HEAD_PAD), blk(GLA_DV),
                  _full(gn.shape)],
        out_specs=blk(GLA_DV),
        out_shape=jax.ShapeDtypeStruct((n, GLA_HEADS * GLA_DV), BF16),
        scratch_shapes=[pltpu.VMEM((seq, GLA_DV), F32), pltpu.VMEM((seq, GLA_DV), F32)],
        compiler_params=_params("parallel", "parallel"),
        name="gla",
    )(gq, gk, gv, gf, gb, sg, gn)


def _mix_kernel(x_ref, a_ref, g_ref, woa_ref, wob_ref, g2_ref, h_ref, xn_ref):
    h = x_ref[...] + _dot(a_ref[...], woa_ref[...]) + _dot(g_ref[...], wob_ref[...])
    h_ref[...] = h
    xn_ref[...] = _rms(h, g2_ref[...]).astype(BF16)


def _mix(x2, attn, ogla, wp, tm):
    n = x2.shape[0]
    tok = lambda w: pl.BlockSpec((tm, w), lambda i: (i, 0))
    return pl.pallas_call(
        _mix_kernel,
        grid=(n // tm,),
        in_specs=[tok(D_MODEL), tok(attn.shape[1]), tok(ogla.shape[1]),
                  _full(wp["woa"].shape), _full(wp["wob"].shape), _full(wp["g2"].shape)],
        out_specs=[tok(D_MODEL), tok(D_MODEL)],
        out_shape=[jax.ShapeDtypeStruct((n, D_MODEL), F32), jax.ShapeDtypeStruct((n, D_MODEL), BF16)],
        compiler_params=_params("parallel"),
        name="mix",
    )(x2, attn, ogla, wp["woa"], wp["wob"], wp["g2"])


def _top_values(s, count):
    cur = jnp.full((1, s.shape[1]), jnp.inf, F32)
    vals = []
    for _ in range(count):
        cur = jnp.max(jnp.where(s < cur, s, -jnp.inf), axis=0, keepdims=True)
        vals.append(cur)
    return vals


def _dup_bf16(x):
    w = pltpu.bitcast(x.astype(BF16).astype(F32), ROW_WORD)
    return w | (w >> 16)


def _packed_row(word):
    return pltpu.bitcast(jnp.broadcast_to(word, (PACK // 2, word.shape[1])), BF16)


def _select_kernel(xn_ref, wqt_ref, sk_ref, r2_ref, e2_ref, e1_ref, cnt_ref):
    qt = _dot_nt(wqt_ref[...], xn_ref[...])
    half = PEER_QDIM // 2
    for h in range(PEER_HEADS):
        s, tops = [], []
        for p in range(2):
            r0 = (2 * h + p) * half
            sp = _dot(sk_ref[2 * h + p], qt[r0:r0 + half, :].astype(BF16))
            s.append(sp)
            tops.append(_top_values(sp, PEER_TOPK))
        top1 = jnp.concatenate(tops[0], axis=0)
        top2 = jnp.concatenate(tops[1], axis=0)
        oct_ = PEER_TOPK // 2
        sub = lax.broadcasted_iota(jnp.int32, (oct_, top2.shape[1]), 0)
        rows = [tops[0][0] + top2]
        for a in range(1, oct_):
            rows.append(jnp.where(sub < PEER_TOPK // (a + 1), tops[0][a] + top2[:oct_, :], -jnp.inf))
        tail = top1[oct_:, :] + tops[1][0]
        cand = jnp.concatenate(rows + [tail], axis=0)
        thr = _top_values(cand, PEER_TOPK)[-1]
        best = tops[0][0] + tops[1][0]
        z = jnp.sum(jnp.where(cand >= thr, jnp.exp(cand - best), 0.0), axis=0, keepdims=True)
        counts = [jnp.sum(jnp.where(r >= thr, 1.0, 0.0), axis=0, keepdims=True) for r in rows]
        tail_hit = jnp.where(tail >= thr, 1.0, 0.0)
        counts += [tail_hit[a:a + 1, :] for a in range(PEER_TOPK - oct_)]
        cnt = jnp.zeros_like(s[0])
        for a in reversed(range(PEER_TOPK)):
            cnt = jnp.where(s[0] >= tops[0][a], counts[a], cnt)
        rank2 = jnp.zeros_like(s[1])
        for a in range(PEER_TOPK):
            rank2 = jnp.where(s[1] < tops[1][a], a + 1.0, rank2)
        r2_ref[h] = rank2.astype(BF16)
        e2_ref[h] = (jnp.exp(s[1] - tops[1][0]) * (1.0 / z)).astype(BF16)
        e1_ref[h] = _dup_bf16(jnp.exp(s[0] - tops[0][0]))
        cnt_ref[h] = _dup_bf16(cnt)


def _select(xn, wp, tb):
    n = xn.shape[0]
    blk = pl.BlockSpec((PEER_HEADS, PEER_NKEYS, tb), lambda i: (0, 0, i))
    shp = lambda dt: jax.ShapeDtypeStruct((PEER_HEADS, PEER_NKEYS, n), dt)
    return pl.pallas_call(
        _select_kernel,
        grid=(n // tb,),
        in_specs=[pl.BlockSpec((tb, D_MODEL), lambda i: (i, 0)),
                  _full(wp["wqt"].shape), _full(wp["subk"].shape)],
        out_specs=[blk, blk, blk, blk],
        out_shape=[shp(BF16), shp(BF16), shp(ROW_WORD), shp(ROW_WORD)],
        compiler_params=_params("parallel"),
        name="peer_select",
    )(xn, wp["wqt"], wp["subk"])


def _peer_kernel(xn_ref, h_ref, r2_ref, e2_ref, e1_ref, cnt_ref, u_ref, vt_ref, gfin_ref,
                 o_ref, acc_ref, g_ref, *, rows_per_chunk):
    c = pl.program_id(1)

    @pl.when(c == 0)
    def _():
        acc_ref[...] = jnp.zeros_like(acc_ref)

    act = _gelu_tanh(_dot_nt(u_ref[...], xn_ref[...]))

    for j in range(rows_per_chunk):
        e1 = [_packed_row(e1_ref[h, j:j + 1, :]) for h in range(PEER_HEADS)]
        cnt = [_packed_row(cnt_ref[h, j:j + 1, :]) for h in range(PEER_HEADS)]
        for m in range(PEER_NKEYS // PACK):
            keys = slice(m * PACK, (m + 1) * PACK)
            w = None
            for h in range(PEER_HEADS):
                term = jnp.where(r2_ref[h, keys, :] < cnt[h], e2_ref[h, keys, :], 0.0) * e1[h]
                w = term if w is None else w + term
            rows = slice(j * PEER_NKEYS + m * PACK, j * PEER_NKEYS + (m + 1) * PACK)
            g_ref[rows, :] = act[rows, :].astype(BF16) * w
    acc_ref[...] += _dot(vt_ref[...], g_ref[...])

    @pl.when(c == pl.num_programs(1) - 1)
    def _():
        y = h_ref[...] + acc_ref[...].T
        o_ref[...] = _rms(y, gfin_ref[...])


def _peer(xn, h, sel, wp, tb, ec):
    n = xn.shape[0]
    r2, e2, e1, cnt = sel
    rpc = ec // PEER_NKEYS
    tok = lambda: pl.BlockSpec((tb, D_MODEL), lambda i, c: (i, 0))
    part = pl.BlockSpec((PEER_HEADS, rpc, tb), lambda i, c: (0, c, i))
    whole = pl.BlockSpec((PEER_HEADS, PEER_NKEYS, tb), lambda i, c: (0, 0, i))
    return pl.pallas_call(
        functools.partial(_peer_kernel, rows_per_chunk=rpc),
        grid=(n // tb, PEER_NEXP // ec),
        in_specs=[tok(), tok(), whole, whole, part, part,
                  pl.BlockSpec((ec, D_MODEL), lambda i, c: (c, 0)),
                  pl.BlockSpec((D_MODEL, ec), lambda i, c: (0, c)),
                  _full(wp["gfin"].shape)],
        out_specs=tok(),
        out_shape=jax.ShapeDtypeStruct((n, D_MODEL), F32),
        scratch_shapes=[pltpu.VMEM((D_MODEL, tb), F32), pltpu.VMEM((ec, tb), BF16)],
        compiler_params=_params("parallel", "arbitrary"),
        name="peer_dense",
    )(xn, h, r2, e2, e1, cnt, wp["u"], wp["vt"], wp["gfin"])


def _pad_heads(w, heads, width):
    r = w.shape[0]
    w = w.reshape(r, heads, width)
    return jnp.pad(w, ((0, 0), (0, 0), (0, HEAD_PAD - width))).reshape(r, heads * HEAD_PAD)


def _rot_cols(w):
    half = MLA_ROPE // 2
    return jnp.concatenate([-w[..., half:], w[..., :half]], axis=-1)


def _prep(seq, norm1_g, w_in, q_norm_g, w_uq, kv_norm_g, w_ukv, gate_fwd_w, gate_fwd_b,
          gate_bwd_w, gate_bwd_b, gla_norm_g, w_o, norm2_g, peer_wq, peer_subkeys, peer_u, peer_v,
          final_norm_g):
    row = lambda g: g.reshape(1, -1).astype(F32)
    o = 0
    segs = {}
    for name, width in (("cq", MLA_Q_LORA), ("ckv", MLA_KV_LORA), ("kr", MLA_ROPE),
                        ("gq", GLA_HEADS * GLA_DK), ("gk", GLA_HEADS * GLA_DK), ("gv", GLA_HEADS * GLA_DV),
                        ("lf", GLA_GATE_RANK), ("lb", GLA_GATE_RANK), ("go", GLA_HEADS * GLA_DV)):
        segs[name] = w_in[:, o:o + width]
        o += width
    misc = jnp.concatenate([segs["kr"], _rot_cols(segs["kr"]), segs["lf"], segs["lb"],
                            jnp.zeros((D_MODEL, LANES - _M_GB - GLA_GATE_RANK), F32)], axis=1)
    w1 = jnp.concatenate([segs["cq"], segs["ckv"],
                          _pad_heads(segs["gq"] * (GLA_DK ** -0.5), GLA_HEADS, GLA_DK),
                          _pad_heads(segs["gk"], GLA_HEADS, GLA_DK),
                          segs["gv"], segs["go"], misc], axis=1)

    wq = w_uq.reshape(MLA_Q_LORA, MLA_HEADS, MLA_NOPE + MLA_ROPE)
    zq = jnp.zeros((MLA_Q_LORA, MLA_HEADS, HEAD_PAD - MLA_NOPE - MLA_ROPE), F32)
    wqa = jnp.concatenate([wq, zq], axis=-1).reshape(MLA_Q_LORA, -1)
    wqb = jnp.concatenate([jnp.zeros_like(wq[..., :MLA_NOPE]), _rot_cols(wq[..., MLA_NOPE:]), zq],
                          axis=-1).reshape(MLA_Q_LORA, -1)
    wkv = w_ukv.reshape(MLA_KV_LORA, MLA_HEADS, MLA_NOPE + MLA_V)
    wka = _pad_heads(wkv[..., :MLA_NOPE].reshape(MLA_KV_LORA, -1), MLA_HEADS, MLA_NOPE)
    wv = _pad_heads(wkv[..., MLA_NOPE:].reshape(MLA_KV_LORA, -1), MLA_HEADS, MLA_V)

    lane = jnp.arange(MLA_HEADS * HEAD_PAD) % HEAD_PAD
    src = jnp.arange(LANES)[:, None]
    in_rope = (lane >= MLA_NOPE) & (lane < MLA_NOPE + MLA_ROPE)
    pa = ((src == (lane - MLA_NOPE + _M_KR)[None, :]) & in_rope[None, :]).astype(BF16)
    pb = ((src == (lane - MLA_NOPE + _M_KROT)[None, :]) & in_rope[None, :]).astype(BF16)

    def gate_w(w, b, lane0):
        wpad = _pad_heads(w, GLA_HEADS, GLA_DK)
        full = jnp.zeros((LANES, wpad.shape[1]), F32).at[lane0:lane0 + GLA_GATE_RANK].set(wpad)
        return full.astype(BF16), _pad_heads(b.reshape(1, -1), GLA_HEADS, GLA_DK)

    gfw, gfb = gate_w(gate_fwd_w, gate_fwd_b, _M_GF)
    gbw, gbb = gate_w(gate_bwd_w, gate_bwd_b, _M_GB)

    half = MLA_ROPE // 2
    freqs = ROPE_THETA ** (-jnp.arange(half, dtype=F32) * 2.0 / MLA_ROPE)
    ang = jnp.arange(seq, dtype=F32)[:, None] * freqs[None, :]
    cos, sin = jnp.cos(ang), jnp.sin(ang)
    zpad = jnp.zeros((seq, HEAD_PAD - MLA_NOPE - MLA_ROPE), F32)
    ck = jnp.concatenate([jnp.ones((seq, MLA_NOPE), F32), cos, cos, zpad], axis=1)
    sk = jnp.concatenate([jnp.zeros((seq, MLA_NOPE), F32), sin, sin, zpad], axis=1)
    scale = (MLA_NOPE + MLA_ROPE) ** -0.5 * 1.4426950408889634

    woa = w_o[:MLA_HEADS * MLA_V].reshape(MLA_HEADS, MLA_V, D_MODEL)
    woa = jnp.pad(woa, ((0, 0), (0, HEAD_PAD - MLA_V), (0, 0))).reshape(MLA_HEADS * HEAD_PAD, D_MODEL)

    return dict(
        g1=row(norm1_g), w1=w1.astype(BF16), gq=row(q_norm_g), gkv=row(kv_norm_g),
        wqa=wqa.astype(BF16), wqb=wqb.astype(BF16), wka=wka.astype(BF16), wv=wv.astype(BF16),
        pa=pa, pb=pb, gfw=gfw, gfb=gfb, gbw=gbw, gbb=gbb,
        cq=ck * scale, sq=sk * scale, ck=ck, sk=sk,
        gn=row(gla_norm_g), woa=woa.astype(BF16), wob=w_o[MLA_HEADS * MLA_V:].astype(BF16),
        g2=row(norm2_g), wqt=peer_wq.T.astype(BF16),
        subk=peer_subkeys.reshape(2 * PEER_HEADS, PEER_NKEYS, PEER_QDIM // 2).astype(BF16),
        u=peer_u.astype(BF16), vt=peer_v.T.astype(BF16), gfin=row(final_norm_g),
    )


def _tile(n, want):
    t = min(n, want)
    assert n % t == 0, (n, want)
    return t


def _trunk(x, wp):
    b, seq, d = x.shape
    x2 = x.reshape(b * seq, d)
    n = b * seq
    tm = _tile(seq, 512)
    q, k, v, gq, gk, gv, gf, gb, sg = _proj(x2, wp, seq, tm)
    attn = _attn(q, k, v, seq, _tile(seq, 2048))
    ogla = _gla(gq, gk, gv, gf, gb, sg, wp["gn"], seq)
    h, xn = _mix(x2, attn, ogla, wp, tm)
    sel = _select(xn, wp, _tile(n, 256))
    y = _peer(xn, h, sel, wp, _tile(n, 1024), 1024)
    return y.reshape(b, seq, d)


def kernel(x_prompt, x_sample, norm1_g, w_in, q_norm_g, w_uq, kv_norm_g, w_ukv, gate_fwd_w, gate_fwd_b,
           gate_bwd_w, gate_bwd_b, gla_norm_g, w_o, norm2_g, peer_wq, peer_subkeys, peer_u, peer_v,
           final_norm_g):
    assert norm1_g.shape[0] == 1, "single layer trunk"
    assert x_prompt.shape[1] == x_sample.shape[1]
    wp = _prep(x_prompt.shape[1], norm1_g[0], w_in[0], q_norm_g[0], w_uq[0], kv_norm_g[0], w_ukv[0],
               gate_fwd_w[0], gate_fwd_b[0], gate_bwd_w[0], gate_bwd_b[0], gla_norm_g[0], w_o[0],
               norm2_g[0], peer_wq[0], peer_subkeys[0], peer_u[0], peer_v[0], final_norm_g)
    return _trunk(x_prompt, wp), _trunk(x_sample, wp)
```

```python
import functools

import jax
import jax.numpy as jnp
from jax import lax
from jax.experimental import pallas as pl
from jax.experimental.pallas import tpu as pltpu

D_MODEL = 1024
MLA_HEADS = 8
MLA_Q_LORA = 256
MLA_KV_LORA = 128
MLA_NOPE = 64
MLA_ROPE = 32
MLA_V = 64
ROPE_THETA = 10000.0
GLA_HEADS = 4
GLA_DK = 64
GLA_DV = 128
GLA_GATE_RANK = 16
GLA_GATE_NORM = 16.0
GLA_CHUNK = 64
GLA_GROUP = 4
PEER_HEADS = 8
PEER_NKEYS = 128
PEER_NEXP = PEER_NKEYS * PEER_NKEYS
PEER_QDIM = 256
PEER_TOPK = 16
EPS = 1e-6

LANES = 128
HEAD_PAD = LANES
VMEM_LIMIT_BYTES = 56 * 1024 * 1024

_C_Q = 0
_C_KV = _C_Q + MLA_Q_LORA
_C_GQ = _C_KV + MLA_KV_LORA
_C_GK = _C_GQ + GLA_HEADS * HEAD_PAD
_C_GV = _C_GK + GLA_HEADS * HEAD_PAD
_C_GO = _C_GV + GLA_HEADS * GLA_DV
_C_MISC = _C_GO + GLA_HEADS * GLA_DV
_IN_COLS = _C_MISC + LANES
_M_KR = 0
_M_KROT = MLA_ROPE
_M_GF = 2 * MLA_ROPE
_M_GB = _M_GF + GLA_GATE_RANK

BF16 = jnp.bfloat16
F32 = jnp.float32
PACK = 16
ROW_WORD = jnp.uint32


def _dot(a, b):
    return jnp.dot(a, b, preferred_element_type=F32)


def _dot_nt(a, b):
    return lax.dot_general(a, b, (((1,), (1,)), ((), ())), preferred_element_type=F32)


def _dot_tn(a, b):
    return lax.dot_general(a, b, (((0,), (0,)), ((), ())), preferred_element_type=F32)


def _rms(x, g):
    return x * lax.rsqrt(jnp.mean(x * x, axis=-1, keepdims=True) + EPS) * g


def _log_sigmoid(x):
    return jnp.minimum(x, 0.0) - jnp.log(1.0 + jnp.exp(-jnp.abs(x)))


def _gelu_tanh(x):
    k = -2.0 * 0.7978845608028654 * 1.4426950408889634
    return x * (1.0 / (1.0 + jnp.exp2(x * (k + (k * 0.044715) * (x * x)))))


def _params(*sem):
    return pltpu.CompilerParams(dimension_semantics=sem, vmem_limit_bytes=VMEM_LIMIT_BYTES)


def _full(shape):
    nd = len(shape)
    return pl.BlockSpec(shape, lambda *_: (0,) * nd)


def _proj_kernel(x_ref, g1_ref, w1_ref, gq_ref, gkv_ref, wqa_ref, wqb_ref, wka_ref, wv_ref,
                 pa_ref, pb_ref, gfw_ref, gfb_ref, gbw_ref, gbb_ref,
                 cq_ref, sq_ref, ck_ref, sk_ref,
                 q_ref, k_ref, v_ref, gq_o, gk_o, gv_o, gf_o, gb_o, sg_o):
    x = x_ref[...]
    n1 = _rms(x, g1_ref[...]).astype(BF16)
    p = _dot(n1, w1_ref[...])
    cqn = _rms(p[:, _C_Q:_C_Q + MLA_Q_LORA], gq_ref[...]).astype(BF16)
    ckvn = _rms(p[:, _C_KV:_C_KV + MLA_KV_LORA], gkv_ref[...]).astype(BF16)
    misc = p[:, _C_MISC:_C_MISC + LANES].astype(BF16)

    qa = _dot(cqn, wqa_ref[...])
    qb = _dot(cqn, wqb_ref[...])
    ka = _dot(ckvn, wka_ref[...]) + _dot(misc, pa_ref[...])
    kb = _dot(misc, pb_ref[...])
    cq, sq, ck, sk = cq_ref[...], sq_ref[...], ck_ref[...], sk_ref[...]
    for h in range(MLA_HEADS):
        sl = slice(h * HEAD_PAD, (h + 1) * HEAD_PAD)
        q_ref[:, sl] = (qa[:, sl] * cq + qb[:, sl] * sq).astype(BF16)
        k_ref[:, sl] = (ka[:, sl] * ck + kb[:, sl] * sk).astype(BF16)
    v_ref[...] = _dot(ckvn, wv_ref[...]).astype(BF16)

    w = GLA_HEADS * HEAD_PAD
    gq_o[...] = p[:, _C_GQ:_C_GQ + w].astype(BF16)
    gk_o[...] = p[:, _C_GK:_C_GK + w].astype(BF16)
    gv_o[...] = p[:, _C_GV:_C_GV + GLA_HEADS * GLA_DV].astype(BF16)
    go = p[:, _C_GO:_C_GO + GLA_HEADS * GLA_DV]
    sg_o[...] = (go * (1.0 / (1.0 + jnp.exp(-go)))).astype(BF16)
    gf_o[...] = _log_sigmoid(_dot(misc, gfw_ref[...]) + gfb_ref[...]) * (1.0 / GLA_GATE_NORM)
    gb_o[...] = _log_sigmoid(_dot(misc, gbw_ref[...]) + gbb_ref[...]) * (1.0 / GLA_GATE_NORM)


def _proj(x2, wp, seq, tm):
    n = x2.shape[0]
    tiles_per_seq = seq // tm
    tok = lambda w: pl.BlockSpec((tm, w), lambda i: (i, 0))
    tab = pl.BlockSpec((tm, HEAD_PAD), lambda i: (i % tiles_per_seq, 0))
    weights = [wp["g1"], wp["w1"], wp["gq"], wp["gkv"], wp["wqa"], wp["wqb"], wp["wka"], wp["wv"],
               wp["pa"], wp["pb"], wp["gfw"], wp["gfb"], wp["gbw"], wp["gbb"]]
    tables = [wp["cq"], wp["sq"], wp["ck"], wp["sk"]]
    hp = MLA_HEADS * HEAD_PAD
    gw = GLA_HEADS * HEAD_PAD
    gv = GLA_HEADS * GLA_DV
    out_w = [(hp, BF16), (hp, BF16), (hp, BF16), (gw, BF16), (gw, BF16), (gv, BF16),
             (gw, F32), (gw, F32), (gv, BF16)]
    return pl.pallas_call(
        _proj_kernel,
        grid=(n // tm,),
        in_specs=[tok(D_MODEL)] + [_full(w.shape) for w in weights] + [tab] * 4,
        out_specs=[tok(w) for w, _ in out_w],
        out_shape=[jax.ShapeDtypeStruct((n, w), dt) for w, dt in out_w],
        compiler_params=_params("parallel"),
        name="proj",
    )(x2, *weights, *tables)


ATTN_SUB = 256


def _attn_kernel(q_ref, k_ref, v_ref, o_ref):
    k = k_ref[...]
    v = v_ref[...]
    subs = [slice(i * ATTN_SUB, (i + 1) * ATTN_SUB) for i in range(q_ref.shape[0] // ATTN_SUB)]
    scores = [_dot_nt(q_ref[r, :], k) for r in subs]
    for r, s in zip(subs, scores):
        p = jnp.exp2(s - jnp.max(s, axis=-1, keepdims=True))
        l = jnp.sum(p, axis=-1, keepdims=True)
        o_ref[r, :] = (_dot(p.astype(BF16), v) * (1.0 / l)).astype(BF16)


def _attn(q, k, v, seq, tq):
    n = q.shape[0]
    nb = n // seq
    nq = seq // tq
    return pl.pallas_call(
        _attn_kernel,
        grid=(nb, MLA_HEADS, nq),
        in_specs=[pl.BlockSpec((tq, HEAD_PAD), lambda b, h, i: (b * nq + i, h)),
                  pl.BlockSpec((seq, HEAD_PAD), lambda b, h, i: (b, h)),
                  pl.BlockSpec((seq, HEAD_PAD), lambda b, h, i: (b, h))],
        out_specs=pl.BlockSpec((tq, HEAD_PAD), lambda b, h, i: (b * nq + i, h)),
        out_shape=jax.ShapeDtypeStruct((n, MLA_HEADS * HEAD_PAD), BF16),
        compiler_params=_params("parallel", "parallel", "arbitrary"),
        name="attn",
    )(q, k, v)


GLA_HEADS_PER_STEP = 2


def _gla_kernel(q_ref, k_ref, v_ref, gf_ref, gb_ref, sg_ref, gn_ref, o_ref, of_ref, ob_ref, *, seq):
    c = GLA_CHUNK
    span = c * GLA_GROUP
    ntrip = seq // span
    row = lax.broadcasted_iota(jnp.int32, (span, span), 0)
    col = lax.broadcasted_iota(jnp.int32, (span, span), 1)
    same_chunk = (row // c) == (col // c)
    causal_f = same_chunk & (row >= col)
    causal_b = same_chunk & (row <= col)
    heads = [slice(i * HEAD_PAD, (i + 1) * HEAD_PAD) for i in range(GLA_HEADS_PER_STEP)]

    def stage_local(rows, head, g_ref, causal, edge):
        q = q_ref[rows, head].astype(F32)
        k = k_ref[rows, head].astype(F32)
        v = v_ref[rows, head]
        g = g_ref[rows, head]
        tri = jnp.where(causal, 1.0, 0.0).astype(BF16)
        hi = g.astype(BF16)
        lo = (g - hi.astype(F32)).astype(BF16)
        b = _dot(tri, hi) + _dot(tri, lo)
        edges = [b[j * c + edge:j * c + edge + 1, :] for j in range(GLA_GROUP)]
        b_edge = jnp.concatenate([jnp.broadcast_to(e, (c, HEAD_PAD)) for e in edges], axis=0)
        q_dec = (q * jnp.exp(b)).astype(BF16)
        k_inc = (k * jnp.exp(-b)).astype(BF16)
        k_dec = (k * jnp.exp(b_edge - b)).astype(BF16)
        att = jnp.where(causal, _dot_nt(q_dec, k_inc), 0.0).astype(BF16)
        o_intra = _dot(att, v)
        kv_t = [_dot_tn(v[j * c:(j + 1) * c, :], k_dec[j * c:(j + 1) * c, :]) for j in range(GLA_GROUP)]
        decay = [jnp.exp(e) for e in edges]
        return q_dec, o_intra, kv_t, decay

    def stage_state(order, local, state_t):
        q_dec, o_intra, kv_t, decay = local
        outs = [None] * GLA_GROUP
        for j in order:
            rows = slice(j * c, (j + 1) * c)
            outs[j] = o_intra[rows, :] + _dot_nt(q_dec[rows, :], state_t.astype(BF16))
            state_t = state_t * decay[j] + kv_t[j]
        return jnp.concatenate(outs, axis=0), state_t

    def body(i, carry):
        rf = pl.ds(pl.multiple_of(i * span, span), span)
        rb = pl.ds(pl.multiple_of((ntrip - 1 - i) * span, span), span)
        local_f = [stage_local(rf, hd, gf_ref, causal_f, c - 1) for hd in heads]
        local_b = [stage_local(rb, hd, gb_ref, causal_b, 0) for hd in heads]
        new = []
        for n, hd in enumerate(heads):
            o_f, sf = stage_state(range(GLA_GROUP), local_f[n], carry[2 * n])
            o_b, sb = stage_state(reversed(range(GLA_GROUP)), local_b[n], carry[2 * n + 1])
            of_ref[rf, hd] = o_f
            ob_ref[rb, hd] = o_b
            new += [sf, sb]
        return tuple(new)

    zero = jnp.zeros((GLA_DV, HEAD_PAD), F32)
    lax.fori_loop(0, ntrip, body, (zero,) * (2 * GLA_HEADS_PER_STEP))
    for hd in heads:
        o = of_ref[:, hd] + ob_ref[:, hd]
        o_ref[:, hd] = (_rms(o, gn_ref[...]) * sg_ref[:, hd].astype(F32)).astype(BF16)


def _gla(gq, gk, gv, gf, gb, sg, gn, seq):
    n = gq.shape[0]
    nb = n // seq
    assert HEAD_PAD == GLA_DV
    width = GLA_HEADS_PER_STEP * GLA_DV
    blk = pl.BlockSpec((seq, width), lambda b, h: (b, h))
    return pl.pallas_call(
        functools.partial(_gla_kernel, seq=seq),
        grid=(nb, GLA_HEADS // GLA_HEADS_PER_STEP),
        in_specs=[blk] * 6 + [_full(gn.shape)],
        out_specs=blk,
        out_shape=jax.ShapeDtypeStruct((n, GLA_HEADS * GLA_DV), BF16),
        scratch_shapes=[pltpu.VMEM((seq, width), F32), pltpu.VMEM((seq, width), F32)],
        compiler_params=_params("parallel", "parallel"),
        name="gla",
    )(gq, gk, gv, gf, gb, sg, gn)


def _mix_kernel(x_ref, a_ref, g_ref, woa_ref, wob_ref, g2_ref, h_ref, xn_ref):
    h = x_ref[...] + _dot(a_ref[...], woa_ref[...]) + _dot(g_ref[...], wob_ref[...])
    h_ref[...] = h
    xn_ref[...] = _rms(h, g2_ref[...]).astype(BF16)


def _mix(x2, attn, ogla, wp, tm):
    n = x2.shape[0]
    tok = lambda w: pl.BlockSpec((tm, w), lambda i: (i, 0))
    return pl.pallas_call(
        _mix_kernel,
        grid=(n // tm,),
        in_specs=[tok(D_MODEL), tok(attn.shape[1]), tok(ogla.shape[1]),
                  _full(wp["woa"].shape), _full(wp["wob"].shape), _full(wp["g2"].shape)],
        out_specs=[tok(D_MODEL), tok(D_MODEL)],
        out_shape=[jax.ShapeDtypeStruct((n, D_MODEL), F32), jax.ShapeDtypeStruct((n, D_MODEL), BF16)],
        compiler_params=_params("parallel"),
        name="mix",
    )(x2, attn, ogla, wp["woa"], wp["wob"], wp["g2"])


def _top_values(s, count):
    cur = jnp.full((1, s.shape[1]), jnp.inf, F32)
    vals = []
    for _ in range(count):
        cur = jnp.max(jnp.where(s < cur, s, -jnp.inf), axis=0, keepdims=True)
        vals.append(cur)
    return vals


def _dup_bf16(x):
    w = pltpu.bitcast(x.astype(BF16).astype(F32), ROW_WORD)
    return w | (w >> 16)


def _packed_row(word):
    return pltpu.bitcast(jnp.broadcast_to(word, (PACK // 2, word.shape[1])), BF16)


def _select_kernel(xn_ref, wqt_ref, sk_ref, r2_ref, e2_ref, e1_ref, cnt_ref):
    qt = _dot_nt(wqt_ref[...], xn_ref[...])
    half = PEER_QDIM // 2
    for h in range(PEER_HEADS):
        s, tops = [], []
        for p in range(2):
            r0 = (2 * h + p) * half
            sp = _dot(sk_ref[2 * h + p], qt[r0:r0 + half, :].astype(BF16))
            s.append(sp)
            tops.append(_top_values(sp, PEER_TOPK))
        top1 = jnp.concatenate(tops[0], axis=0)
        top2 = jnp.concatenate(tops[1], axis=0)
        oct_ = PEER_TOPK // 2
        sub = lax.broadcasted_iota(jnp.int32, (oct_, top2.shape[1]), 0)
        rows = [tops[0][0] + top2]
        for a in range(1, oct_):
            rows.append(jnp.where(sub < PEER_TOPK // (a + 1), tops[0][a] + top2[:oct_, :], -jnp.inf))
        tail = top1[oct_:, :] + tops[1][0]
        cand = jnp.concatenate(rows + [tail], axis=0)
        thr = _top_values(cand, PEER_TOPK)[-1]
        best = tops[0][0] + tops[1][0]
        z = jnp.sum(jnp.where(cand >= thr, jnp.exp(cand - best), 0.0), axis=0, keepdims=True)
        counts = [jnp.sum(jnp.where(r >= thr, 1.0, 0.0), axis=0, keepdims=True) for r in rows]
        tail_hit = jnp.where(tail >= thr, 1.0, 0.0)
        counts += [tail_hit[a:a + 1, :] for a in range(PEER_TOPK - oct_)]
        cnt = jnp.zeros_like(s[0])
        for a in reversed(range(PEER_TOPK)):
            cnt = jnp.where(s[0] >= tops[0][a], counts[a], cnt)
        rank2 = jnp.zeros_like(s[1])
        for a in range(PEER_TOPK):
            rank2 = jnp.where(s[1] < tops[1][a], a + 1.0, rank2)
        r2_ref[h] = rank2.astype(BF16)
        e2_ref[h] = (jnp.exp(s[1] - tops[1][0]) * (1.0 / z)).astype(BF16)
        e1_ref[h] = _dup_bf16(jnp.exp(s[0] - tops[0][0]))
        cnt_ref[h] = _dup_bf16(cnt)


def _select(xn, wp, tb):
    n = xn.shape[0]
    blk = pl.BlockSpec((PEER_HEADS, PEER_NKEYS, tb), lambda i: (0, 0, i))
    shp = lambda dt: jax.ShapeDtypeStruct((PEER_HEADS, PEER_NKEYS, n), dt)
    return pl.pallas_call(
        _select_kernel,
        grid=(n // tb,),
        in_specs=[pl.BlockSpec((tb, D_MODEL), lambda i: (i, 0)),
                  _full(wp["wqt"].shape), _full(wp["subk"].shape)],
        out_specs=[blk, blk, blk, blk],
        out_shape=[shp(BF16), shp(BF16), shp(ROW_WORD), shp(ROW_WORD)],
        compiler_params=_params("parallel"),
        name="peer_select",
    )(xn, wp["wqt"], wp["subk"])


def _peer_kernel(xn_ref, h_ref, r2_ref, e2_ref, e1_ref, cnt_ref, u_ref, vt_ref, gfin_ref,
                 o_ref, acc_ref, g_ref, *, rows_per_chunk):
    c = pl.program_id(1)

    @pl.when(c == 0)
    def _():
        acc_ref[...] = jnp.zeros_like(acc_ref)

    act = _gelu_tanh(_dot_nt(u_ref[...], xn_ref[...]).astype(BF16))

    for j in range(rows_per_chunk):
        e1 = [_packed_row(e1_ref[h, j:j + 1, :]) for h in range(PEER_HEADS)]
        cnt = [_packed_row(cnt_ref[h, j:j + 1, :]) for h in range(PEER_HEADS)]
        for m in range(PEER_NKEYS // PACK):
            keys = slice(m * PACK, (m + 1) * PACK)
            w = None
            for h in range(PEER_HEADS):
                term = jnp.where(r2_ref[h, keys, :] < cnt[h], e2_ref[h, keys, :], 0.0) * e1[h]
                w = term if w is None else w + term
            rows = slice(j * PEER_NKEYS + m * PACK, j * PEER_NKEYS + (m + 1) * PACK)
            g_ref[rows, :] = act[rows, :] * w
    acc_ref[...] += _dot(vt_ref[...], g_ref[...])

    @pl.when(c == pl.num_programs(1) - 1)
    def _():
        y = h_ref[...] + acc_ref[...].T
        o_ref[...] = _rms(y, gfin_ref[...])


def _peer(xn, h, sel, wp, tb, ec):
    n = xn.shape[0]
    r2, e2, e1, cnt = sel
    rpc = ec // PEER_NKEYS
    tok = lambda: pl.BlockSpec((tb, D_MODEL), lambda i, c: (i, 0))
    part = pl.BlockSpec((PEER_HEADS, rpc, tb), lambda i, c: (0, c, i))
    whole = pl.BlockSpec((PEER_HEADS, PEER_NKEYS, tb), lambda i, c: (0, 0, i))
    return pl.pallas_call(
        functools.partial(_peer_kernel, rows_per_chunk=rpc),
        grid=(n // tb, PEER_NEXP // ec),
        in_specs=[tok(), tok(), whole, whole, part, part,
                  pl.BlockSpec((ec, D_MODEL), lambda i, c: (c, 0)),
                  pl.BlockSpec((D_MODEL, ec), lambda i, c: (0, c)),
                  _full(wp["gfin"].shape)],
        out_specs=tok(),
        out_shape=jax.ShapeDtypeStruct((n, D_MODEL), F32),
        scratch_shapes=[pltpu.VMEM((D_MODEL, tb), F32), pltpu.VMEM((ec, tb), BF16)],
        compiler_params=_params("parallel", "arbitrary"),
        name="peer_dense",
    )(xn, h, r2, e2, e1, cnt, wp["u"], wp["vt"], wp["gfin"])


def _pad_heads(w, heads, width):
    r = w.shape[0]
    w = w.reshape(r, heads, width)
    return jnp.pad(w, ((0, 0), (0, 0), (0, HEAD_PAD - width))).reshape(r, heads * HEAD_PAD)


def _rot_cols(w):
    half = MLA_ROPE // 2
    return jnp.concatenate([-w[..., half:], w[..., :half]], axis=-1)


def _prep(seq, norm1_g, w_in, q_norm_g, w_uq, kv_norm_g, w_ukv, gate_fwd_w, gate_fwd_b,
          gate_bwd_w, gate_bwd_b, gla_norm_g, w_o, norm2_g, peer_wq, peer_subkeys, peer_u, peer_v,
          final_norm_g):
    row = lambda g: g.reshape(1, -1).astype(F32)
    o = 0
    segs = {}
    for name, width in (("cq", MLA_Q_LORA), ("ckv", MLA_KV_LORA), ("kr", MLA_ROPE),
                        ("gq", GLA_HEADS * GLA_DK), ("gk", GLA_HEADS * GLA_DK), ("gv", GLA_HEADS * GLA_DV),
                        ("lf", GLA_GATE_RANK), ("lb", GLA_GATE_RANK), ("go", GLA_HEADS * GLA_DV)):
        segs[name] = w_in[:, o:o + width]
        o += width
    misc = jnp.concatenate([segs["kr"], _rot_cols(segs["kr"]), segs["lf"], segs["lb"],
                            jnp.zeros((D_MODEL, LANES - _M_GB - GLA_GATE_RANK), F32)], axis=1)
    w1 = jnp.concatenate([segs["cq"], segs["ckv"],
                          _pad_heads(segs["gq"] * (GLA_DK ** -0.5), GLA_HEADS, GLA_DK),
                          _pad_heads(segs["gk"], GLA_HEADS, GLA_DK),
                          segs["gv"], segs["go"], misc], axis=1)

    wq = w_uq.reshape(MLA_Q_LORA, MLA_HEADS, MLA_NOPE + MLA_ROPE)
    zq = jnp.zeros((MLA_Q_LORA, MLA_HEADS, HEAD_PAD - MLA_NOPE - MLA_ROPE), F32)
    wqa = jnp.concatenate([wq, zq], axis=-1).reshape(MLA_Q_LORA, -1)
    wqb = jnp.concatenate([jnp.zeros_like(wq[..., :MLA_NOPE]), _rot_cols(wq[..., MLA_NOPE:]), zq],
                          axis=-1).reshape(MLA_Q_LORA, -1)
    wkv = w_ukv.reshape(MLA_KV_LORA, MLA_HEADS, MLA_NOPE + MLA_V)
    wka = _pad_heads(wkv[..., :MLA_NOPE].reshape(MLA_KV_LORA, -1), MLA_HEADS, MLA_NOPE)
    wv = _pad_heads(wkv[..., MLA_NOPE:].reshape(MLA_KV_LORA, -1), MLA_HEADS, MLA_V)

    lane = jnp.arange(MLA_HEADS * HEAD_PAD) % HEAD_PAD
    src = jnp.arange(LANES)[:, None]
    in_rope = (lane >= MLA_NOPE) & (lane < MLA_NOPE + MLA_ROPE)
    pa = ((src == (lane - MLA_NOPE + _M_KR)[None, :]) & in_rope[None, :]).astype(BF16)
    pb = ((src == (lane - MLA_NOPE + _M_KROT)[None, :]) & in_rope[None, :]).astype(BF16)

    def gate_w(w, b, lane0):
        wpad = _pad_heads(w, GLA_HEADS, GLA_DK)
        full = jnp.zeros((LANES, wpad.shape[1]), F32).at[lane0:lane0 + GLA_GATE_RANK].set(wpad)
        return full.astype(BF16), _pad_heads(b.reshape(1, -1), GLA_HEADS, GLA_DK)

    gfw, gfb = gate_w(gate_fwd_w, gate_fwd_b, _M_GF)
    gbw, gbb = gate_w(gate_bwd_w, gate_bwd_b, _M_GB)

    half = MLA_ROPE // 2
    freqs = ROPE_THETA ** (-jnp.arange(half, dtype=F32) * 2.0 / MLA_ROPE)
    ang = jnp.arange(seq, dtype=F32)[:, None] * freqs[None, :]
    cos, sin = jnp.cos(ang), jnp.sin(ang)
    zpad = jnp.zeros((seq, HEAD_PAD - MLA_NOPE - MLA_ROPE), F32)
    ck = jnp.concatenate([jnp.ones((seq, MLA_NOPE), F32), cos, cos, zpad], axis=1)
    sk = jnp.concatenate([jnp.zeros((seq, MLA_NOPE), F32), sin, sin, zpad], axis=1)
    scale = (MLA_NOPE + MLA_ROPE) ** -0.5 * 1.4426950408889634

    woa = w_o[:MLA_HEADS * MLA_V].reshape(MLA_HEADS, MLA_V, D_MODEL)
    woa = jnp.pad(woa, ((0, 0), (0, HEAD_PAD - MLA_V), (0, 0))).reshape(MLA_HEADS * HEAD_PAD, D_MODEL)

    return dict(
        g1=row(norm1_g), w1=w1.astype(BF16), gq=row(q_norm_g), gkv=row(kv_norm_g),
        wqa=wqa.astype(BF16), wqb=wqb.astype(BF16), wka=wka.astype(BF16), wv=wv.astype(BF16),
        pa=pa, pb=pb, gfw=gfw, gfb=gfb, gbw=gbw, gbb=gbb,
        cq=ck * scale, sq=sk * scale, ck=ck, sk=sk,
        gn=row(gla_norm_g), woa=woa.astype(BF16), wob=w_o[MLA_HEADS * MLA_V:].astype(BF16),
        g2=row(norm2_g), wqt=peer_wq.T.astype(BF16),
        subk=peer_subkeys.reshape(2 * PEER_HEADS, PEER_NKEYS, PEER_QDIM // 2).astype(BF16),
        u=peer_u.astype(BF16), vt=peer_v.T.astype(BF16), gfin=row(final_norm_g),
    )


def _tile(n, want):
    t = min(n, want)
    assert n % t == 0, (n, want)
    return t


def _trunk(x, wp):
    b, seq, d = x.shape
    x2 = x.reshape(b * seq, d)
    n = b * seq
    tm = _tile(seq, 512)
    q, k, v, gq, gk, gv, gf, gb, sg = _proj(x2, wp, seq, tm)
    attn = _attn(q, k, v, seq, _tile(seq, 2048))
    ogla = _gla(gq, gk, gv, gf, gb, sg, wp["gn"], seq)
    h, xn = _mix(x2, attn, ogla, wp, tm)
    sel = _select(xn, wp, _tile(n, 256))
    y = _peer(xn, h, sel, wp, _tile(n, 1024), 1024)
    return y.reshape(b, seq, d)


def kernel(x_prompt, x_sample, norm1_g, w_in, q_norm_g, w_uq, kv_norm_g, w_ukv, gate_fwd_w, gate_fwd_b,
           gate_bwd_w, gate_bwd_b, gla_norm_g, w_o, norm2_g, peer_wq, peer_subkeys, peer_u, peer_v,
           final_norm_g):
    assert norm1_g.shape[0] == 1, "single layer trunk"
    assert x_prompt.shape[1] == x_sample.shape[1]
    wp = _prep(x_prompt.shape[1], norm1_g[0], w_in[0], q_norm_g[0], w_uq[0], kv_norm_g[0], w_ukv[0],
               gate_fwd_w[0], gate_fwd_b[0], gate_bwd_w[0], gate_bwd_b[0], gla_norm_g[0], w_o[0],
               norm2_g[0], peer_wq[0], peer_subkeys[0], peer_u[0], peer_v[0], final_norm_g)
    return _trunk(x_prompt, wp), _trunk(x_sample, wp)
```

```python
import functools

import jax
import jax.numpy as jnp
from jax import lax
from jax.experimental import pallas as pl
from jax.experimental.pallas import tpu as pltpu

D_MODEL = 1024
MLA_HEADS = 8
MLA_Q_LORA = 256
MLA_KV_LORA = 128
MLA_NOPE = 64
MLA_ROPE = 32
MLA_V = 64
ROPE_THETA = 10000.0
GLA_HEADS = 4
GLA_DK = 64
GLA_DV = 128
GLA_GATE_RANK = 16
GLA_GATE_NORM = 16.0
GLA_CHUNK = 64
GLA_GROUP = 4
PEER_HEADS = 8
PEER_NKEYS = 128
PEER_NEXP = PEER_NKEYS * PEER_NKEYS
PEER_QDIM = 256
PEER_TOPK = 16
EPS = 1e-6

LANES = 128
HEAD_PAD = LANES
VMEM_LIMIT_BYTES = 56 * 1024 * 1024

_C_Q = 0
_C_KV = _C_Q + MLA_Q_LORA
_C_GQ = _C_KV + MLA_KV_LORA
_C_GK = _C_GQ + GLA_HEADS * HEAD_PAD
_C_GV = _C_GK + GLA_HEADS * HEAD_PAD
_C_GO = _C_GV + GLA_HEADS * GLA_DV
_C_MISC = _C_GO + GLA_HEADS * GLA_DV
_IN_COLS = _C_MISC + LANES
_M_KR = 0
_M_KROT = MLA_ROPE
_M_GF = 2 * MLA_ROPE
_M_GB = _M_GF + GLA_GATE_RANK

BF16 = jnp.bfloat16
F32 = jnp.float32
PACK = 16
ROW_WORD = jnp.uint32


def _dot(a, b):
    return jnp.dot(a, b, preferred_element_type=F32)


def _dot_nt(a, b):
    return lax.dot_general(a, b, (((1,), (1,)), ((), ())), preferred_element_type=F32)


def _dot_tn(a, b):
    return lax.dot_general(a, b, (((0,), (0,)), ((), ())), preferred_element_type=F32)


def _rms(x, g):
    return x * lax.rsqrt(jnp.mean(x * x, axis=-1, keepdims=True) + EPS) * g


def _log_sigmoid(x):
    return jnp.minimum(x, 0.0) - jnp.log(1.0 + jnp.exp(-jnp.abs(x)))


def _gelu_tanh(x):
    k = -2.0 * 0.7978845608028654 * 1.4426950408889634
    return x * (1.0 / (1.0 + jnp.exp2(x * (k + (k * 0.044715) * (x * x)))))


def _params(*sem):
    return pltpu.CompilerParams(dimension_semantics=sem, vmem_limit_bytes=VMEM_LIMIT_BYTES)


def _full(shape):
    nd = len(shape)
    return pl.BlockSpec(shape, lambda *_: (0,) * nd)


def _proj_kernel(x_ref, g1_ref, w1_ref, gq_ref, gkv_ref, wqa_ref, wqb_ref, wka_ref, wv_ref,
                 pa_ref, pb_ref, gfw_ref, gfb_ref, gbw_ref, gbb_ref,
                 cq_ref, sq_ref, ck_ref, sk_ref,
                 q_ref, k_ref, v_ref, gq_o, gk_o, gv_o, gf_o, gb_o, sg_o):
    x = x_ref[...]
    n1 = _rms(x, g1_ref[...]).astype(BF16)
    p = _dot(n1, w1_ref[...])
    cqn = _rms(p[:, _C_Q:_C_Q + MLA_Q_LORA], gq_ref[...]).astype(BF16)
    ckvn = _rms(p[:, _C_KV:_C_KV + MLA_KV_LORA], gkv_ref[...]).astype(BF16)
    misc = p[:, _C_MISC:_C_MISC + LANES].astype(BF16)

    qa = _dot(cqn, wqa_ref[...])
    qb = _dot(cqn, wqb_ref[...])
    ka = _dot(ckvn, wka_ref[...]) + _dot(misc, pa_ref[...])
    kb = _dot(misc, pb_ref[...])
    cq, sq, ck, sk = cq_ref[...], sq_ref[...], ck_ref[...], sk_ref[...]
    for h in range(MLA_HEADS):
        sl = slice(h * HEAD_PAD, (h + 1) * HEAD_PAD)
        q_ref[:, sl] = (qa[:, sl] * cq + qb[:, sl] * sq).astype(BF16)
        k_ref[:, sl] = (ka[:, sl] * ck + kb[:, sl] * sk).astype(BF16)
    lane = lax.broadcasted_iota(jnp.int32, (1, MLA_HEADS * HEAD_PAD), 1)
    ones_lane = jnp.where(lane % HEAD_PAD == MLA_V, 1.0, 0.0)
    v_ref[...] = (_dot(ckvn, wv_ref[...]) + ones_lane).astype(BF16)

    w = GLA_HEADS * HEAD_PAD
    gq_o[...] = p[:, _C_GQ:_C_GQ + w].astype(BF16)
    gk_o[...] = p[:, _C_GK:_C_GK + w].astype(BF16)
    gv_o[...] = p[:, _C_GV:_C_GV + GLA_HEADS * GLA_DV].astype(BF16)
    go = p[:, _C_GO:_C_GO + GLA_HEADS * GLA_DV]
    sg_o[...] = (go * (1.0 / (1.0 + jnp.exp(-go)))).astype(BF16)
    gf_o[...] = _log_sigmoid(_dot(misc, gfw_ref[...]) + gfb_ref[...]) * (1.0 / GLA_GATE_NORM)
    gb_o[...] = _log_sigmoid(_dot(misc, gbw_ref[...]) + gbb_ref[...]) * (1.0 / GLA_GATE_NORM)


def _proj(x2, wp, seq, tm):
    n = x2.shape[0]
    tiles_per_seq = seq // tm
    tok = lambda w: pl.BlockSpec((tm, w), lambda i: (i, 0))
    tab = pl.BlockSpec((tm, HEAD_PAD), lambda i: (i % tiles_per_seq, 0))
    weights = [wp["g1"], wp["w1"], wp["gq"], wp["gkv"], wp["wqa"], wp["wqb"], wp["wka"], wp["wv"],
               wp["pa"], wp["pb"], wp["gfw"], wp["gfb"], wp["gbw"], wp["gbb"]]
    tables = [wp["cq"], wp["sq"], wp["ck"], wp["sk"]]
    hp = MLA_HEADS * HEAD_PAD
    gw = GLA_HEADS * HEAD_PAD
    gv = GLA_HEADS * GLA_DV
    out_w = [(hp, BF16), (hp, BF16), (hp, BF16), (gw, BF16), (gw, BF16), (gv, BF16),
             (gw, F32), (gw, F32), (gv, BF16)]
    return pl.pallas_call(
        _proj_kernel,
        grid=(n // tm,),
        in_specs=[tok(D_MODEL)] + [_full(w.shape) for w in weights] + [tab] * 4,
        out_specs=[tok(w) for w, _ in out_w],
        out_shape=[jax.ShapeDtypeStruct((n, w), dt) for w, dt in out_w],
        compiler_params=_params("parallel"),
        name="proj",
    )(x2, *weights, *tables)


ATTN_SUB = 256


def _attn_kernel(q_ref, k_ref, v_ref, o_ref):
    k = k_ref[...]
    v = v_ref[...]
    subs = [slice(i * ATTN_SUB, (i + 1) * ATTN_SUB) for i in range(q_ref.shape[0] // ATTN_SUB)]
    scores = [_dot_nt(q_ref[r, :], k) for r in subs]
    for r, s in zip(subs, scores):
        p = jnp.exp2(s - jnp.max(s, axis=-1, keepdims=True)).astype(BF16)
        o = _dot(p, v)
        o_ref[r, :] = (o * (1.0 / o[:, MLA_V:MLA_V + 1])).astype(BF16)


def _attn(q, k, v, seq, tq):
    n = q.shape[0]
    nb = n // seq
    nq = seq // tq
    return pl.pallas_call(
        _attn_kernel,
        grid=(nb, MLA_HEADS, nq),
        in_specs=[pl.BlockSpec((tq, HEAD_PAD), lambda b, h, i: (b * nq + i, h)),
                  pl.BlockSpec((seq, HEAD_PAD), lambda b, h, i: (b, h)),
                  pl.BlockSpec((seq, HEAD_PAD), lambda b, h, i: (b, h))],
        out_specs=pl.BlockSpec((tq, HEAD_PAD), lambda b, h, i: (b * nq + i, h)),
        out_shape=jax.ShapeDtypeStruct((n, MLA_HEADS * HEAD_PAD), BF16),
        compiler_params=_params("parallel", "parallel", "arbitrary"),
        name="attn",
    )(q, k, v)


GLA_HEADS_PER_STEP = 2


def _gla_kernel(q_ref, k_ref, v_ref, gf_ref, gb_ref, sg_ref, gn_ref, o_ref, of_ref, ob_ref, *, seq):
    c = GLA_CHUNK
    span = c * GLA_GROUP
    ntrip = seq // span
    row = lax.broadcasted_iota(jnp.int32, (span, span), 0)
    col = lax.broadcasted_iota(jnp.int32, (span, span), 1)
    same_chunk = (row // c) == (col // c)
    causal_f = same_chunk & (row >= col)
    causal_b = same_chunk & (row <= col)
    heads = [slice(i * HEAD_PAD, (i + 1) * HEAD_PAD) for i in range(GLA_HEADS_PER_STEP)]

    def stage_local(rows, head, g_ref, causal, edge):
        q = q_ref[rows, head].astype(F32)
        k = k_ref[rows, head].astype(F32)
        v = v_ref[rows, head]
        g = g_ref[rows, head]
        tri = jnp.where(causal, 1.0, 0.0).astype(BF16)
        hi = g.astype(BF16)
        lo = (g - hi.astype(F32)).astype(BF16)
        b = _dot(tri, hi) + _dot(tri, lo)
        edges = [b[j * c + edge:j * c + edge + 1, :] for j in range(GLA_GROUP)]
        b_edge = jnp.concatenate([jnp.broadcast_to(e, (c, HEAD_PAD)) for e in edges], axis=0)
        q_dec = (q * jnp.exp(b)).astype(BF16)
        k_inc = (k * jnp.exp(-b)).astype(BF16)
        k_dec = (k * jnp.exp(b_edge - b)).astype(BF16)
        att = jnp.where(causal, _dot_nt(q_dec, k_inc), 0.0).astype(BF16)
        o_intra = _dot(att, v)
        kv_t = [_dot_tn(v[j * c:(j + 1) * c, :], k_dec[j * c:(j + 1) * c, :]) for j in range(GLA_GROUP)]
        decay = [jnp.exp(e) for e in edges]
        return q_dec, o_intra, kv_t, decay

    def stage_state(order, local, state_t):
        q_dec, o_intra, kv_t, decay = local
        outs = [None] * GLA_GROUP
        for j in order:
            rows = slice(j * c, (j + 1) * c)
            outs[j] = o_intra[rows, :] + _dot_nt(q_dec[rows, :], state_t.astype(BF16))
            state_t = state_t * decay[j] + kv_t[j]
        return jnp.concatenate(outs, axis=0), state_t

    def body(i, carry):
        rf = pl.ds(pl.multiple_of(i * span, span), span)
        rb = pl.ds(pl.multiple_of((ntrip - 1 - i) * span, span), span)
        local_f = [stage_local(rf, hd, gf_ref, causal_f, c - 1) for hd in heads]
        local_b = [stage_local(rb, hd, gb_ref, causal_b, 0) for hd in heads]
        new = []
        for n, hd in enumerate(heads):
            o_f, sf = stage_state(range(GLA_GROUP), local_f[n], carry[2 * n])
            o_b, sb = stage_state(reversed(range(GLA_GROUP)), local_b[n], carry[2 * n + 1])
            of_ref[rf, hd] = o_f
            ob_ref[rb, hd] = o_b
            new += [sf, sb]
        return tuple(new)

    zero = jnp.zeros((GLA_DV, HEAD_PAD), F32)
    lax.fori_loop(0, ntrip, body, (zero,) * (2 * GLA_HEADS_PER_STEP))
    for hd in heads:
        o = of_ref[:, hd] + ob_ref[:, hd]
        o_ref[:, hd] = (_rms(o, gn_ref[...]) * sg_ref[:, hd].astype(F32)).astype(BF16)


def _gla(gq, gk, gv, gf, gb, sg, gn, seq):
    n = gq.shape[0]
    nb = n // seq
    assert HEAD_PAD == GLA_DV
    width = GLA_HEADS_PER_STEP * GLA_DV
    blk = pl.BlockSpec((seq, width), lambda b, h: (b, h))
    return pl.pallas_call(
        functools.partial(_gla_kernel, seq=seq),
        grid=(nb, GLA_HEADS // GLA_HEADS_PER_STEP),
        in_specs=[blk] * 6 + [_full(gn.shape)],
        out_specs=blk,
        out_shape=jax.ShapeDtypeStruct((n, GLA_HEADS * GLA_DV), BF16),
        scratch_shapes=[pltpu.VMEM((seq, width), F32), pltpu.VMEM((seq, width), F32)],
        compiler_params=_params("parallel", "parallel"),
        name="gla",
    )(gq, gk, gv, gf, gb, sg, gn)


def _mix_kernel(x_ref, a_ref, g_ref, woa_ref, wob_ref, g2_ref, h_ref, xn_ref):
    h = x_ref[...] + _dot(a_ref[...], woa_ref[...]) + _dot(g_ref[...], wob_ref[...])
    h_ref[...] = h
    xn_ref[...] = _rms(h, g2_ref[...]).astype(BF16)


def _mix(x2, attn, ogla, wp, tm):
    n = x2.shape[0]
    tok = lambda w: pl.BlockSpec((tm, w), lambda i: (i, 0))
    return pl.pallas_call(
        _mix_kernel,
        grid=(n // tm,),
        in_specs=[tok(D_MODEL), tok(attn.shape[1]), tok(ogla.shape[1]),
                  _full(wp["woa"].shape), _full(wp["wob"].shape), _full(wp["g2"].shape)],
        out_specs=[tok(D_MODEL), tok(D_MODEL)],
        out_shape=[jax.ShapeDtypeStruct((n, D_MODEL), F32), jax.ShapeDtypeStruct((n, D_MODEL), BF16)],
        compiler_params=_params("parallel"),
        name="mix",
    )(x2, attn, ogla, wp["woa"], wp["wob"], wp["g2"])


def _sort_pairs(n):
    pairs = []
    p = 1
    while p < n:
        k = p
        while k >= 1:
            for j in range(k % p, n - k, 2 * k):
                for i in range(min(k, n - j - k)):
                    if (i + j) // (2 * p) == (i + j + k) // (2 * p):
                        pairs.append((i + j, i + j + k))
            k //= 2
        p *= 2
    return pairs


def _top_values(s, count):
    sub = 8
    groups = s.shape[0] // (sub * count)
    v = [s[sub * i:sub * (i + 1), :] for i in range(groups * count)]

    def exchange(i, j):
        v[i], v[j] = jnp.maximum(v[i], v[j]), jnp.minimum(v[i], v[j])

    def merge_halves(lo):
        d = count // 2
        while d >= 1:
            for k in range(count):
                if k & d == 0:
                    exchange(lo + k, lo + k + d)
            d //= 2

    for g in range(groups):
        for i, j in _sort_pairs(count):
            exchange(g * count + i, g * count + j)
    while groups > 1:
        groups //= 2
        for g in range(groups):
            a, b = 2 * g * count, (2 * g + 1) * count
            for k in range(count):
                v[g * count + k] = jnp.maximum(v[a + k], v[b + count - 1 - k])
            merge_halves(g * count)
    shift = sub // 2
    while shift >= 1:
        other = [pltpu.roll(v[count - 1 - k], shift, axis=0) for k in range(count)]
        for k in range(count):
            v[k] = jnp.maximum(v[k], other[k])
        merge_halves(0)
        shift //= 2
    return [v[k][0:1, :] for k in range(count)]


def _dup_bf16(x):
    w = pltpu.bitcast(x.astype(BF16).astype(F32), ROW_WORD)
    return w | (w >> 16)


def _packed_row(word):
    return pltpu.bitcast(jnp.broadcast_to(word, (PACK // 2, word.shape[1])), BF16)


def _select_kernel(xn_ref, wqt_ref, sk_ref, r2_ref, e2_ref, e1_ref, cnt_ref):
    qt = _dot_nt(wqt_ref[...], xn_ref[...])
    half = PEER_QDIM // 2
    for h in range(PEER_HEADS):
        s, tops = [], []
        for p in range(2):
            r0 = (2 * h + p) * half
            sp = _dot(sk_ref[2 * h + p], qt[r0:r0 + half, :].astype(BF16))
            s.append(sp)
            tops.append(_top_values(sp, PEER_TOPK))
        top1 = jnp.concatenate(tops[0], axis=0)
        top2 = jnp.concatenate(tops[1], axis=0)
        oct_ = PEER_TOPK // 2
        sub = lax.broadcasted_iota(jnp.int32, (oct_, top2.shape[1]), 0)
        rows = [tops[0][0] + top2]
        for a in range(1, oct_):
            rows.append(jnp.where(sub < PEER_TOPK // (a + 1), tops[0][a] + top2[:oct_, :], -jnp.inf))
        tail = top1[oct_:, :] + tops[1][0]
        cand = jnp.concatenate(rows + [tail], axis=0)
        pad = jnp.full((-cand.shape[0] % (8 * PEER_TOPK), cand.shape[1]), -jnp.inf, F32)
        thr = _top_values(jnp.concatenate([cand, pad], axis=0), PEER_TOPK)[-1]
        best = tops[0][0] + tops[1][0]
        z = jnp.sum(jnp.where(cand >= thr, jnp.exp(cand - best), 0.0), axis=0, keepdims=True)
        counts = [jnp.sum(jnp.where(r >= thr, 1.0, 0.0), axis=0, keepdims=True) for r in rows]
        tail_hit = jnp.where(tail >= thr, 1.0, 0.0)
        counts += [tail_hit[a:a + 1, :] for a in range(PEER_TOPK - oct_)]
        cnt = jnp.zeros_like(s[0])
        for a in reversed(range(PEER_TOPK)):
            cnt = jnp.where(s[0] >= tops[0][a], counts[a], cnt)
        rank2 = jnp.zeros_like(s[1])
        for a in range(PEER_TOPK):
            rank2 = jnp.where(s[1] < tops[1][a], a + 1.0, rank2)
        r2_ref[h] = rank2.astype(BF16)
        e2_ref[h] = (jnp.exp(s[1] - tops[1][0]) * (1.0 / z)).astype(BF16)
        e1_ref[h] = _dup_bf16(jnp.exp(s[0] - tops[0][0]))
        cnt_ref[h] = _dup_bf16(cnt)


def _select(xn, wp, tb):
    n = xn.shape[0]
    blk = pl.BlockSpec((PEER_HEADS, PEER_NKEYS, tb), lambda i: (0, 0, i))
    shp = lambda dt: jax.ShapeDtypeStruct((PEER_HEADS, PEER_NKEYS, n), dt)
    return pl.pallas_call(
        _select_kernel,
        grid=(n // tb,),
        in_specs=[pl.BlockSpec((tb, D_MODEL), lambda i: (i, 0)),
                  _full(wp["wqt"].shape), _full(wp["subk"].shape)],
        out_specs=[blk, blk, blk, blk],
        out_shape=[shp(BF16), shp(BF16), shp(ROW_WORD), shp(ROW_WORD)],
        compiler_params=_params("parallel"),
        name="peer_select",
    )(xn, wp["wqt"], wp["subk"])


def _peer_kernel(xn_ref, h_ref, r2_ref, e2_ref, e1_ref, cnt_ref, u_ref, vt_ref, gfin_ref,
                 o_ref, acc_ref, g_ref, *, rows_per_chunk):
    c = pl.program_id(1)

    @pl.when(c == 0)
    def _():
        acc_ref[...] = jnp.zeros_like(acc_ref)

    act = _gelu_tanh(_dot_nt(u_ref[...], xn_ref[...]).astype(BF16))

    for j in range(rows_per_chunk):
        e1 = [_packed_row(e1_ref[h, j:j + 1, :]) for h in range(PEER_HEADS)]
        cnt = [_packed_row(cnt_ref[h, j:j + 1, :]) for h in range(PEER_HEADS)]
        for m in range(PEER_NKEYS // PACK):
            keys = slice(m * PACK, (m + 1) * PACK)
            w = None
            for h in range(PEER_HEADS):
                term = jnp.where(r2_ref[h, keys, :] < cnt[h], e2_ref[h, keys, :], 0.0) * e1[h]
                w = term if w is None else w + term
            rows = slice(j * PEER_NKEYS + m * PACK, j * PEER_NKEYS + (m + 1) * PACK)
            g_ref[rows, :] = act[rows, :] * w
    acc_ref[...] += _dot(vt_ref[...], g_ref[...])

    @pl.when(c == pl.num_programs(1) - 1)
    def _():
        y = h_ref[...] + acc_ref[...].T
        o_ref[...] = _rms(y, gfin_ref[...])


def _peer(xn, h, sel, wp, tb, ec):
    n = xn.shape[0]
    r2, e2, e1, cnt = sel
    rpc = ec // PEER_NKEYS
    tok = lambda: pl.BlockSpec((tb, D_MODEL), lambda i, c: (i, 0))
    part = pl.BlockSpec((PEER_HEADS, rpc, tb), lambda i, c: (0, c, i))
    whole = pl.BlockSpec((PEER_HEADS, PEER_NKEYS, tb), lambda i, c: (0, 0, i))
    return pl.pallas_call(
        functools.partial(_peer_kernel, rows_per_chunk=rpc),
        grid=(n // tb, PEER_NEXP // ec),
        in_specs=[tok(), tok(), whole, whole, part, part,
                  pl.BlockSpec((ec, D_MODEL), lambda i, c: (c, 0)),
                  pl.BlockSpec((D_MODEL, ec), lambda i, c: (0, c)),
                  _full(wp["gfin"].shape)],
        out_specs=tok(),
        out_shape=jax.ShapeDtypeStruct((n, D_MODEL), F32),
        scratch_shapes=[pltpu.VMEM((D_MODEL, tb), F32), pltpu.VMEM((ec, tb), BF16)],
        compiler_params=_params("parallel", "arbitrary"),
        name="peer_dense",
    )(xn, h, r2, e2, e1, cnt, wp["u"], wp["vt"], wp["gfin"])


def _pad_heads(w, heads, width):
    r = w.shape[0]
    w = w.reshape(r, heads, width)
    return jnp.pad(w, ((0, 0), (0, 0), (0, HEAD_PAD - width))).reshape(r, heads * HEAD_PAD)


def _rot_cols(w):
    half = MLA_ROPE // 2
    return jnp.concatenate([-w[..., half:], w[..., :half]], axis=-1)


def _prep(seq, norm1_g, w_in, q_norm_g, w_uq, kv_norm_g, w_ukv, gate_fwd_w, gate_fwd_b,
          gate_bwd_w, gate_bwd_b, gla_norm_g, w_o, norm2_g, peer_wq, peer_subkeys, peer_u, peer_v,
          final_norm_g):
    row = lambda g: g.reshape(1, -1).astype(F32)
    o = 0
    segs = {}
    for name, width in (("cq", MLA_Q_LORA), ("ckv", MLA_KV_LORA), ("kr", MLA_ROPE),
                        ("gq", GLA_HEADS * GLA_DK), ("gk", GLA_HEADS * GLA_DK), ("gv", GLA_HEADS * GLA_DV),
                        ("lf", GLA_GATE_RANK), ("lb", GLA_GATE_RANK), ("go", GLA_HEADS * GLA_DV)):
        segs[name] = w_in[:, o:o + width]
        o += width
    misc = jnp.concatenate([segs["kr"], _rot_cols(segs["kr"]), segs["lf"], segs["lb"],
                            jnp.zeros((D_MODEL, LANES - _M_GB - GLA_GATE_RANK), F32)], axis=1)
    w1 = jnp.concatenate([segs["cq"], segs["ckv"],
                          _pad_heads(segs["gq"] * (GLA_DK ** -0.5), GLA_HEADS, GLA_DK),
                          _pad_heads(segs["gk"], GLA_HEADS, GLA_DK),
                          segs["gv"], segs["go"], misc], axis=1)

    wq = w_uq.reshape(MLA_Q_LORA, MLA_HEADS, MLA_NOPE + MLA_ROPE)
    zq = jnp.zeros((MLA_Q_LORA, MLA_HEADS, HEAD_PAD - MLA_NOPE - MLA_ROPE), F32)
    wqa = jnp.concatenate([wq, zq], axis=-1).reshape(MLA_Q_LORA, -1)
    wqb = jnp.concatenate([jnp.zeros_like(wq[..., :MLA_NOPE]), _rot_cols(wq[..., MLA_NOPE:]), zq],
                          axis=-1).reshape(MLA_Q_LORA, -1)
    wkv = w_ukv.reshape(MLA_KV_LORA, MLA_HEADS, MLA_NOPE + MLA_V)
    wka = _pad_heads(wkv[..., :MLA_NOPE].reshape(MLA_KV_LORA, -1), MLA_HEADS, MLA_NOPE)
    wv = _pad_heads(wkv[..., MLA_NOPE:].reshape(MLA_KV_LORA, -1), MLA_HEADS, MLA_V)

    lane = jnp.arange(MLA_HEADS * HEAD_PAD) % HEAD_PAD
    src = jnp.arange(LANES)[:, None]
    in_rope = (lane >= MLA_NOPE) & (lane < MLA_NOPE + MLA_ROPE)
    pa = ((src == (lane - MLA_NOPE + _M_KR)[None, :]) & in_rope[None, :]).astype(BF16)
    pb = ((src == (lane - MLA_NOPE + _M_KROT)[None, :]) & in_rope[None, :]).astype(BF16)

    def gate_w(w, b, lane0):
        wpad = _pad_heads(w, GLA_HEADS, GLA_DK)
        full = jnp.zeros((LANES, wpad.shape[1]), F32).at[lane0:lane0 + GLA_GATE_RANK].set(wpad)
        return full.astype(BF16), _pad_heads(b.reshape(1, -1), GLA_HEADS, GLA_DK)

    gfw, gfb = gate_w(gate_fwd_w, gate_fwd_b, _M_GF)
    gbw, gbb = gate_w(gate_bwd_w, gate_bwd_b, _M_GB)

    half = MLA_ROPE // 2
    freqs = ROPE_THETA ** (-jnp.arange(half, dtype=F32) * 2.0 / MLA_ROPE)
    ang = jnp.arange(seq, dtype=F32)[:, None] * freqs[None, :]
    cos, sin = jnp.cos(ang), jnp.sin(ang)
    zpad = jnp.zeros((seq, HEAD_PAD - MLA_NOPE - MLA_ROPE), F32)
    ck = jnp.concatenate([jnp.ones((seq, MLA_NOPE), F32), cos, cos, zpad], axis=1)
    sk = jnp.concatenate([jnp.zeros((seq, MLA_NOPE), F32), sin, sin, zpad], axis=1)
    scale = (MLA_NOPE + MLA_ROPE) ** -0.5 * 1.4426950408889634

    woa = w_o[:MLA_HEADS * MLA_V].reshape(MLA_HEADS, MLA_V, D_MODEL)
    woa = jnp.pad(woa, ((0, 0), (0, HEAD_PAD - MLA_V), (0, 0))).reshape(MLA_HEADS * HEAD_PAD, D_MODEL)

    return dict(
        g1=row(norm1_g), w1=w1.astype(BF16), gq=row(q_norm_g), gkv=row(kv_norm_g),
        wqa=wqa.astype(BF16), wqb=wqb.astype(BF16), wka=wka.astype(BF16), wv=wv.astype(BF16),
        pa=pa, pb=pb, gfw=gfw, gfb=gfb, gbw=gbw, gbb=gbb,
        cq=ck * scale, sq=sk * scale, ck=ck, sk=sk,
        gn=row(gla_norm_g), woa=woa.astype(BF16), wob=w_o[MLA_HEADS * MLA_V:].astype(BF16),
        g2=row(norm2_g), wqt=peer_wq.T.astype(BF16),
        subk=peer_subkeys.reshape(2 * PEER_HEADS, PEER_NKEYS, PEER_QDIM // 2).astype(BF16),
        u=peer_u.astype(BF16), vt=peer_v.T.astype(BF16), gfin=row(final_norm_g),
    )


def _tile(n, want):
    t = min(n, want)
    assert n % t == 0, (n, want)
    return t


def _trunk(x, wp):
    b, seq, d = x.shape
    x2 = x.reshape(b * seq, d)
    n = b * seq
    tm = _tile(seq, 512)
    q, k, v, gq, gk, gv, gf, gb, sg = _proj(x2, wp, seq, tm)
    attn = _attn(q, k, v, seq, _tile(seq, 2048))
    ogla = _gla(gq, gk, gv, gf, gb, sg, wp["gn"], seq)
    h, xn = _mix(x2, attn, ogla, wp, tm)
    sel = _select(xn, wp, _tile(n, 256))
    y = _peer(xn, h, sel, wp, _tile(n, 1024), 1024)
    return y.reshape(b, seq, d)


def kernel(x_prompt, x_sample, norm1_g, w_in, q_norm_g, w_uq, kv_norm_g, w_ukv, gate_fwd_w, gate_fwd_b,
           gate_bwd_w, gate_bwd_b, gla_norm_g, w_o, norm2_g, peer_wq, peer_subkeys, peer_u, peer_v,
           final_norm_g):
    assert norm1_g.shape[0] == 1, "single layer trunk"
    assert x_prompt.shape[1] == x_sample.shape[1]
    wp = _prep(x_prompt.shape[1], norm1_g[0], w_in[0], q_norm_g[0], w_uq[0], kv_norm_g[0], w_ukv[0],
               gate_fwd_w[0], gate_fwd_b[0], gate_bwd_w[0], gate_bwd_b[0], gla_norm_g[0], w_o[0],
               norm2_g[0], peer_wq[0], peer_subkeys[0], peer_u[0], peer_v[0], final_norm_g)
    return _trunk(x_prompt, wp), _trunk(x_sample, wp)
```

```python
import functools

import jax
import jax.numpy as jnp
from jax import lax
from jax.experimental import pallas as pl
from jax.experimental.pallas import tpu as pltpu

D_MODEL = 1024
MLA_HEADS = 8
MLA_Q_LORA = 256
MLA_KV_LORA = 128
MLA_NOPE = 64
MLA_ROPE = 32
MLA_V = 64
ROPE_THETA = 10000.0
GLA_HEADS = 4
GLA_DK = 64
GLA_DV = 128
GLA_GATE_RANK = 16
GLA_GATE_NORM = 16.0
GLA_CHUNK = 64
GLA_GROUP = 4
PEER_HEADS = 8
PEER_NKEYS = 128
PEER_NEXP = PEER_NKEYS * PEER_NKEYS
PEER_QDIM = 256
PEER_TOPK = 16
EPS = 1e-6

LANES = 128
HEAD_PAD = LANES
VMEM_LIMIT_BYTES = 56 * 1024 * 1024

_C_Q = 0
_C_KV = _C_Q + MLA_Q_LORA
_C_GQ = _C_KV + MLA_KV_LORA
_C_GK = _C_GQ + GLA_HEADS * HEAD_PAD
_C_GV = _C_GK + GLA_HEADS * HEAD_PAD
_C_GO = _C_GV + GLA_HEADS * GLA_DV
_C_MISC = _C_GO + GLA_HEADS * GLA_DV
_IN_COLS = _C_MISC + LANES
_M_KR = 0
_M_KROT = MLA_ROPE
_M_GF = 2 * MLA_ROPE
_M_GB = _M_GF + GLA_GATE_RANK

BF16 = jnp.bfloat16
F32 = jnp.float32
PACK = 16


def _dot(a, b):
    return jnp.dot(a, b, preferred_element_type=F32)


def _dot_nt(a, b):
    return lax.dot_general(a, b, (((1,), (1,)), ((), ())), preferred_element_type=F32)


def _dot_tn(a, b):
    return lax.dot_general(a, b, (((0,), (0,)), ((), ())), preferred_element_type=F32)


def _rms(x, g):
    return x * lax.rsqrt(jnp.mean(x * x, axis=-1, keepdims=True) + EPS) * g


def _log_sigmoid(x):
    return jnp.minimum(x, 0.0) - jnp.log(1.0 + jnp.exp(-jnp.abs(x)))


def _gelu_tanh(x):
    k = -2.0 * 0.7978845608028654 * 1.4426950408889634
    return x * (1.0 / (1.0 + jnp.exp2(x * (k + (k * 0.044715) * (x * x)))))


def _params(*sem):
    return pltpu.CompilerParams(dimension_semantics=sem, vmem_limit_bytes=VMEM_LIMIT_BYTES)


def _full(shape):
    nd = len(shape)
    return pl.BlockSpec(shape, lambda *_: (0,) * nd)


def _proj_kernel(x_ref, g1_ref, w1_ref, gq_ref, gkv_ref, wqa_ref, wqb_ref, wka_ref, wv_ref,
                 pa_ref, pb_ref, gfw_ref, gfb_ref, gbw_ref, gbb_ref,
                 cq_ref, sq_ref, ck_ref, sk_ref,
                 q_ref, k_ref, v_ref, gq_o, gk_o, gv_o, gf_o, gb_o, sg_o):
    x = x_ref[...]
    n1 = _rms(x, g1_ref[...]).astype(BF16)
    p = _dot(n1, w1_ref[...])
    cqn = _rms(p[:, _C_Q:_C_Q + MLA_Q_LORA], gq_ref[...]).astype(BF16)
    ckvn = _rms(p[:, _C_KV:_C_KV + MLA_KV_LORA], gkv_ref[...]).astype(BF16)
    misc = p[:, _C_MISC:_C_MISC + LANES].astype(BF16)

    qa = _dot(cqn, wqa_ref[...])
    qb = _dot(cqn, wqb_ref[...])
    ka = _dot(ckvn, wka_ref[...]) + _dot(misc, pa_ref[...])
    kb = _dot(misc, pb_ref[...])
    cq, sq, ck, sk = cq_ref[...], sq_ref[...], ck_ref[...], sk_ref[...]
    for h in range(MLA_HEADS):
        sl = slice(h * HEAD_PAD, (h + 1) * HEAD_PAD)
        q_ref[:, sl] = (qa[:, sl] * cq + qb[:, sl] * sq).astype(BF16)
        k_ref[:, sl] = (ka[:, sl] * ck + kb[:, sl] * sk).astype(BF16)
    lane = lax.broadcasted_iota(jnp.int32, (1, MLA_HEADS * HEAD_PAD), 1)
    ones_lane = jnp.where(lane % HEAD_PAD == MLA_V, 1.0, 0.0)
    v_ref[...] = (_dot(ckvn, wv_ref[...]) + ones_lane).astype(BF16)

    w = GLA_HEADS * HEAD_PAD
    gq_o[...] = p[:, _C_GQ:_C_GQ + w].astype(BF16)
    gk_o[...] = p[:, _C_GK:_C_GK + w].astype(BF16)
    gv_o[...] = p[:, _C_GV:_C_GV + GLA_HEADS * GLA_DV].astype(BF16)
    go = p[:, _C_GO:_C_GO + GLA_HEADS * GLA_DV]
    sg_o[...] = (go * (1.0 / (1.0 + jnp.exp(-go)))).astype(BF16)
    gf_o[...] = _log_sigmoid(_dot(misc, gfw_ref[...]) + gfb_ref[...]) * (1.0 / GLA_GATE_NORM)
    gb_o[...] = _log_sigmoid(_dot(misc, gbw_ref[...]) + gbb_ref[...]) * (1.0 / GLA_GATE_NORM)


def _proj(x2, wp, seq, tm):
    n = x2.shape[0]
    tiles_per_seq = seq // tm
    tok = lambda w: pl.BlockSpec((tm, w), lambda i: (i, 0))
    tab = pl.BlockSpec((tm, HEAD_PAD), lambda i: (i % tiles_per_seq, 0))
    weights = [wp["g1"], wp["w1"], wp["gq"], wp["gkv"], wp["wqa"], wp["wqb"], wp["wka"], wp["wv"],
               wp["pa"], wp["pb"], wp["gfw"], wp["gfb"], wp["gbw"], wp["gbb"]]
    tables = [wp["cq"], wp["sq"], wp["ck"], wp["sk"]]
    hp = MLA_HEADS * HEAD_PAD
    gw = GLA_HEADS * HEAD_PAD
    gv = GLA_HEADS * GLA_DV
    out_w = [(hp, BF16), (hp, BF16), (hp, BF16), (gw, BF16), (gw, BF16), (gv, BF16),
             (gw, F32), (gw, F32), (gv, BF16)]
    return pl.pallas_call(
        _proj_kernel,
        grid=(n // tm,),
        in_specs=[tok(D_MODEL)] + [_full(w.shape) for w in weights] + [tab] * 4,
        out_specs=[tok(w) for w, _ in out_w],
        out_shape=[jax.ShapeDtypeStruct((n, w), dt) for w, dt in out_w],
        compiler_params=_params("parallel"),
        name="proj",
    )(x2, *weights, *tables)


ATTN_SUB = 256


def _attn_kernel(q_ref, k_ref, v_ref, o_ref):
    k = k_ref[...]
    v = v_ref[...]
    subs = [slice(i * ATTN_SUB, (i + 1) * ATTN_SUB) for i in range(q_ref.shape[0] // ATTN_SUB)]
    scores = [_dot_nt(q_ref[r, :], k) for r in subs]
    for r, s in zip(subs, scores):
        p = jnp.exp2(s - jnp.max(s, axis=-1, keepdims=True)).astype(BF16)
        o = _dot(p, v)
        o_ref[r, :] = (o * (1.0 / o[:, MLA_V:MLA_V + 1])).astype(BF16)


def _attn(q, k, v, seq, tq):
    n = q.shape[0]
    nb = n // seq
    nq = seq // tq
    return pl.pallas_call(
        _attn_kernel,
        grid=(nb, MLA_HEADS, nq),
        in_specs=[pl.BlockSpec((tq, HEAD_PAD), lambda b, h, i: (b * nq + i, h)),
                  pl.BlockSpec((seq, HEAD_PAD), lambda b, h, i: (b, h)),
                  pl.BlockSpec((seq, HEAD_PAD), lambda b, h, i: (b, h))],
        out_specs=pl.BlockSpec((tq, HEAD_PAD), lambda b, h, i: (b * nq + i, h)),
        out_shape=jax.ShapeDtypeStruct((n, MLA_HEADS * HEAD_PAD), BF16),
        compiler_params=_params("parallel", "parallel", "arbitrary"),
        name="attn",
    )(q, k, v)


GLA_HEADS_PER_STEP = 2


def _gla_kernel(q_ref, k_ref, v_ref, gf_ref, gb_ref, sg_ref, gn_ref, o_ref, of_ref, ob_ref, *, seq):
    c = GLA_CHUNK
    span = c * GLA_GROUP
    ntrip = seq // span
    row = lax.broadcasted_iota(jnp.int32, (span, span), 0)
    col = lax.broadcasted_iota(jnp.int32, (span, span), 1)
    same_chunk = (row // c) == (col // c)
    causal_f = same_chunk & (row >= col)
    causal_b = same_chunk & (row <= col)
    heads = [slice(i * HEAD_PAD, (i + 1) * HEAD_PAD) for i in range(GLA_HEADS_PER_STEP)]

    def stage_local(rows, head, g_ref, causal, edge):
        q = q_ref[rows, head].astype(F32)
        k = k_ref[rows, head].astype(F32)
        v = v_ref[rows, head]
        g = g_ref[rows, head]
        tri = jnp.where(causal, 1.0, 0.0).astype(BF16)
        hi = g.astype(BF16)
        lo = (g - hi.astype(F32)).astype(BF16)
        b = _dot(tri, hi) + _dot(tri, lo)
        edges = [b[j * c + edge:j * c + edge + 1, :] for j in range(GLA_GROUP)]
        b_edge = jnp.concatenate([jnp.broadcast_to(e, (c, HEAD_PAD)) for e in edges], axis=0)
        q_dec = (q * jnp.exp(b)).astype(BF16)
        k_inc = (k * jnp.exp(-b)).astype(BF16)
        k_dec = (k * jnp.exp(b_edge - b)).astype(BF16)
        att = jnp.where(causal, _dot_nt(q_dec, k_inc), 0.0).astype(BF16)
        o_intra = _dot(att, v)
        kv_t = [_dot_tn(v[j * c:(j + 1) * c, :], k_dec[j * c:(j + 1) * c, :]) for j in range(GLA_GROUP)]
        decay = [jnp.exp(e) for e in edges]
        return q_dec, o_intra, kv_t, decay

    def stage_state(order, local, state_t):
        q_dec, o_intra, kv_t, decay = local
        outs = [None] * GLA_GROUP
        for j in order:
            rows = slice(j * c, (j + 1) * c)
            outs[j] = o_intra[rows, :] + _dot_nt(q_dec[rows, :], state_t.astype(BF16))
            state_t = state_t * decay[j] + kv_t[j]
        return jnp.concatenate(outs, axis=0), state_t

    def body(i, carry):
        rf = pl.ds(pl.multiple_of(i * span, span), span)
        rb = pl.ds(pl.multiple_of((ntrip - 1 - i) * span, span), span)
        local_f = [stage_local(rf, hd, gf_ref, causal_f, c - 1) for hd in heads]
        local_b = [stage_local(rb, hd, gb_ref, causal_b, 0) for hd in heads]
        new = []
        for n, hd in enumerate(heads):
            o_f, sf = stage_state(range(GLA_GROUP), local_f[n], carry[2 * n])
            o_b, sb = stage_state(reversed(range(GLA_GROUP)), local_b[n], carry[2 * n + 1])
            of_ref[rf, hd] = o_f
            ob_ref[rb, hd] = o_b
            new += [sf, sb]
        return tuple(new)

    zero = jnp.zeros((GLA_DV, HEAD_PAD), F32)
    lax.fori_loop(0, ntrip, body, (zero,) * (2 * GLA_HEADS_PER_STEP))
    for hd in heads:
        o = of_ref[:, hd] + ob_ref[:, hd]
        o_ref[:, hd] = (_rms(o, gn_ref[...]) * sg_ref[:, hd].astype(F32)).astype(BF16)


def _gla(gq, gk, gv, gf, gb, sg, gn, seq):
    n = gq.shape[0]
    nb = n // seq
    assert HEAD_PAD == GLA_DV
    width = GLA_HEADS_PER_STEP * GLA_DV
    blk = pl.BlockSpec((seq, width), lambda b, h: (b, h))
    return pl.pallas_call(
        functools.partial(_gla_kernel, seq=seq),
        grid=(nb, GLA_HEADS // GLA_HEADS_PER_STEP),
        in_specs=[blk] * 6 + [_full(gn.shape)],
        out_specs=blk,
        out_shape=jax.ShapeDtypeStruct((n, GLA_HEADS * GLA_DV), BF16),
        scratch_shapes=[pltpu.VMEM((seq, width), F32), pltpu.VMEM((seq, width), F32)],
        compiler_params=_params("parallel", "parallel"),
        name="gla",
    )(gq, gk, gv, gf, gb, sg, gn)


def _mix_kernel(x_ref, a_ref, g_ref, woa_ref, wob_ref, g2_ref, h_ref, xn_ref):
    h = x_ref[...] + _dot(a_ref[...], woa_ref[...]) + _dot(g_ref[...], wob_ref[...])
    h_ref[...] = h
    xn_ref[...] = _rms(h, g2_ref[...]).astype(BF16)


def _mix(x2, attn, ogla, wp, tm):
    n = x2.shape[0]
    tok = lambda w: pl.BlockSpec((tm, w), lambda i: (i, 0))
    return pl.pallas_call(
        _mix_kernel,
        grid=(n // tm,),
        in_specs=[tok(D_MODEL), tok(attn.shape[1]), tok(ogla.shape[1]),
                  _full(wp["woa"].shape), _full(wp["wob"].shape), _full(wp["g2"].shape)],
        out_specs=[tok(D_MODEL), tok(D_MODEL)],
        out_shape=[jax.ShapeDtypeStruct((n, D_MODEL), F32), jax.ShapeDtypeStruct((n, D_MODEL), BF16)],
        compiler_params=_params("parallel"),
        name="mix",
    )(x2, attn, ogla, wp["woa"], wp["wob"], wp["g2"])


def _sort_pairs(n):
    pairs = []
    p = 1
    while p < n:
        k = p
        while k >= 1:
            for j in range(k % p, n - k, 2 * k):
                for i in range(min(k, n - j - k)):
                    if (i + j) // (2 * p) == (i + j + k) // (2 * p):
                        pairs.append((i + j, i + j + k))
            k //= 2
        p *= 2
    return pairs


def _top_values(s, count):
    sub = 8
    groups = s.shape[0] // (sub * count)
    v = [s[sub * i:sub * (i + 1), :] for i in range(groups * count)]

    def exchange(i, j):
        v[i], v[j] = jnp.maximum(v[i], v[j]), jnp.minimum(v[i], v[j])

    def merge_halves(lo):
        d = count // 2
        while d >= 1:
            for k in range(count):
                if k & d == 0:
                    exchange(lo + k, lo + k + d)
            d //= 2

    for g in range(groups):
        for i, j in _sort_pairs(count):
            exchange(g * count + i, g * count + j)
    while groups > 1:
        groups //= 2
        for g in range(groups):
            a, b = 2 * g * count, (2 * g + 1) * count
            for k in range(count):
                v[g * count + k] = jnp.maximum(v[a + k], v[b + count - 1 - k])
            merge_halves(g * count)
    shift = sub // 2
    while shift >= 1:
        other = [pltpu.roll(v[count - 1 - k], shift, axis=0) for k in range(count)]
        for k in range(count):
            v[k] = jnp.maximum(v[k], other[k])
        merge_halves(0)
        shift //= 2
    return [v[k][0:1, :] for k in range(count)]


def _packed_row(row):
    return jnp.broadcast_to(row[None, :], (PACK, row.shape[0])).astype(BF16)


def _select_kernel(xn_ref, wqt_ref, sk_ref, r2_ref, e2_ref, e1_ref, cnt_ref):
    qt = _dot_nt(wqt_ref[...], xn_ref[...])
    half = PEER_QDIM // 2
    for h in range(PEER_HEADS):
        s, tops = [], []
        for p in range(2):
            r0 = (2 * h + p) * half
            sp = _dot(sk_ref[2 * h + p], qt[r0:r0 + half, :].astype(BF16))
            s.append(sp)
            tops.append(_top_values(sp, PEER_TOPK))
        top1 = jnp.concatenate(tops[0], axis=0)
        top2 = jnp.concatenate(tops[1], axis=0)
        oct_ = PEER_TOPK // 2
        sub = lax.broadcasted_iota(jnp.int32, (oct_, top2.shape[1]), 0)
        rows = [tops[0][0] + top2]
        for a in range(1, oct_):
            rows.append(jnp.where(sub < PEER_TOPK // (a + 1), tops[0][a] + top2[:oct_, :], -jnp.inf))
        tail = top1[oct_:, :] + tops[1][0]
        cand = jnp.concatenate(rows + [tail], axis=0)
        pad = jnp.full((-cand.shape[0] % (8 * PEER_TOPK), cand.shape[1]), -jnp.inf, F32)
        thr = _top_values(jnp.concatenate([cand, pad], axis=0), PEER_TOPK)[-1]
        best = tops[0][0] + tops[1][0]
        z = jnp.sum(jnp.where(cand >= thr, jnp.exp(cand - best), 0.0), axis=0, keepdims=True)
        counts = [jnp.sum(jnp.where(r >= thr, 1.0, 0.0), axis=0, keepdims=True) for r in rows]
        tail_hit = jnp.where(tail >= thr, 1.0, 0.0)
        counts += [tail_hit[a:a + 1, :] for a in range(PEER_TOPK - oct_)]
        cnt = jnp.zeros_like(s[0])
        for a in reversed(range(PEER_TOPK)):
            cnt = jnp.where(s[0] >= tops[0][a], counts[a], cnt)
        rank2 = jnp.zeros_like(s[1])
        for a in range(PEER_TOPK):
            rank2 = jnp.where(s[1] < tops[1][a], a + 1.0, rank2)
        r2_ref[h] = rank2.astype(BF16)
        e2_ref[h] = (jnp.exp(s[1] - tops[1][0]) * (1.0 / z)).astype(BF16)
        e1_ref[h] = jnp.exp(s[0] - tops[0][0])
        cnt_ref[h] = cnt


def _select(xn, wp, tb):
    n = xn.shape[0]
    blk = pl.BlockSpec((PEER_HEADS, PEER_NKEYS, tb), lambda i: (0, 0, i))
    shp = lambda dt: jax.ShapeDtypeStruct((PEER_HEADS, PEER_NKEYS, n), dt)
    return pl.pallas_call(
        _select_kernel,
        grid=(n // tb,),
        in_specs=[pl.BlockSpec((tb, D_MODEL), lambda i: (i, 0)),
                  _full(wp["wqt"].shape), _full(wp["subk"].shape)],
        out_specs=[blk, blk, blk, blk],
        out_shape=[shp(BF16), shp(BF16), shp(F32), shp(F32)],
        compiler_params=_params("parallel"),
        name="peer_select",
    )(xn, wp["wqt"], wp["subk"])


def _peer_kernel(xn_ref, h_ref, r2_ref, e2_ref, e1_ref, cnt_ref, u_ref, vt_ref, gfin_ref,
                 o_ref, acc_ref, g_ref, *, rows_per_chunk):
    c = pl.program_id(1)

    @pl.when(c == 0)
    def _():
        acc_ref[...] = jnp.zeros_like(acc_ref)

    act = _gelu_tanh(_dot_nt(u_ref[...], xn_ref[...]).astype(BF16))

    for j in range(rows_per_chunk):
        e1 = [_packed_row(e1_ref[h, j, :]) for h in range(PEER_HEADS)]
        cnt = [_packed_row(cnt_ref[h, j, :]) for h in range(PEER_HEADS)]
        for m in range(PEER_NKEYS // PACK):
            keys = slice(m * PACK, (m + 1) * PACK)
            w = None
            for h in range(PEER_HEADS):
                term = jnp.where(r2_ref[h, keys, :] < cnt[h], e2_ref[h, keys, :], 0.0) * e1[h]
                w = term if w is None else w + term
            rows = slice(j * PEER_NKEYS + m * PACK, j * PEER_NKEYS + (m + 1) * PACK)
            g_ref[rows, :] = act[rows, :] * w
    acc_ref[...] += _dot(vt_ref[...], g_ref[...])

    @pl.when(c == pl.num_programs(1) - 1)
    def _():
        y = h_ref[...] + acc_ref[...].T
        o_ref[...] = _rms(y, gfin_ref[...])


def _peer(xn, h, sel, wp, tb, ec):
    n = xn.shape[0]
    r2, e2, e1, cnt = sel
    rpc = ec // PEER_NKEYS
    tok = lambda: pl.BlockSpec((tb, D_MODEL), lambda i, c: (i, 0))
    part = pl.BlockSpec((PEER_HEADS, rpc, tb), lambda i, c: (0, c, i))
    whole = pl.BlockSpec((PEER_HEADS, PEER_NKEYS, tb), lambda i, c: (0, 0, i))
    return pl.pallas_call(
        functools.partial(_peer_kernel, rows_per_chunk=rpc),
        grid=(n // tb, PEER_NEXP // ec),
        in_specs=[tok(), tok(), whole, whole, part, part,
                  pl.BlockSpec((ec, D_MODEL), lambda i, c: (c, 0)),
                  pl.BlockSpec((D_MODEL, ec), lambda i, c: (0, c)),
                  _full(wp["gfin"].shape)],
        out_specs=tok(),
        out_shape=jax.ShapeDtypeStruct((n, D_MODEL), F32),
        scratch_shapes=[pltpu.VMEM((D_MODEL, tb), F32), pltpu.VMEM((ec, tb), BF16)],
        compiler_params=_params("parallel", "arbitrary"),
        name="peer_dense",
    )(xn, h, r2, e2, e1, cnt, wp["u"], wp["vt"], wp["gfin"])


def _pad_heads(w, heads, width):
    r = w.shape[0]
    w = w.reshape(r, heads, width)
    return jnp.pad(w, ((0, 0), (0, 0), (0, HEAD_PAD - width))).reshape(r, heads * HEAD_PAD)


def _rot_cols(w):
    half = MLA_ROPE // 2
    return jnp.concatenate([-w[..., half:], w[..., :half]], axis=-1)


def _prep(seq, norm1_g, w_in, q_norm_g, w_uq, kv_norm_g, w_ukv, gate_fwd_w, gate_fwd_b,
          gate_bwd_w, gate_bwd_b, gla_norm_g, w_o, norm2_g, peer_wq, peer_subkeys, peer_u, peer_v,
          final_norm_g):
    row = lambda g: g.reshape(1, -1).astype(F32)
    o = 0
    segs = {}
    for name, width in (("cq", MLA_Q_LORA), ("ckv", MLA_KV_LORA), ("kr", MLA_ROPE),
                        ("gq", GLA_HEADS * GLA_DK), ("gk", GLA_HEADS * GLA_DK), ("gv", GLA_HEADS * GLA_DV),
                        ("lf", GLA_GATE_RANK), ("lb", GLA_GATE_RANK), ("go", GLA_HEADS * GLA_DV)):
        segs[name] = w_in[:, o:o + width]
        o += width
    misc = jnp.concatenate([segs["kr"], _rot_cols(segs["kr"]), segs["lf"], segs["lb"],
                            jnp.zeros((D_MODEL, LANES - _M_GB - GLA_GATE_RANK), F32)], axis=1)
    w1 = jnp.concatenate([segs["cq"], segs["ckv"],
                          _pad_heads(segs["gq"] * (GLA_DK ** -0.5), GLA_HEADS, GLA_DK),
                          _pad_heads(segs["gk"], GLA_HEADS, GLA_DK),
                          segs["gv"], segs["go"], misc], axis=1)

    wq = w_uq.reshape(MLA_Q_LORA, MLA_HEADS, MLA_NOPE + MLA_ROPE)
    zq = jnp.zeros((MLA_Q_LORA, MLA_HEADS, HEAD_PAD - MLA_NOPE - MLA_ROPE), F32)
    wqa = jnp.concatenate([wq, zq], axis=-1).reshape(MLA_Q_LORA, -1)
    wqb = jnp.concatenate([jnp.zeros_like(wq[..., :MLA_NOPE]), _rot_cols(wq[..., MLA_NOPE:]), zq],
                          axis=-1).reshape(MLA_Q_LORA, -1)
    wkv = w_ukv.reshape(MLA_KV_LORA, MLA_HEADS, MLA_NOPE + MLA_V)
    wka = _pad_heads(wkv[..., :MLA_NOPE].reshape(MLA_KV_LORA, -1), MLA_HEADS, MLA_NOPE)
    wv = _pad_heads(wkv[..., MLA_NOPE:].reshape(MLA_KV_LORA, -1), MLA_HEADS, MLA_V)

    lane = jnp.arange(MLA_HEADS * HEAD_PAD) % HEAD_PAD
    src = jnp.arange(LANES)[:, None]
    in_rope = (lane >= MLA_NOPE) & (lane < MLA_NOPE + MLA_ROPE)
    pa = ((src == (lane - MLA_NOPE + _M_KR)[None, :]) & in_rope[None, :]).astype(BF16)
    pb = ((src == (lane - MLA_NOPE + _M_KROT)[None, :]) & in_rope[None, :]).astype(BF16)

    def gate_w(w, b, lane0):
        wpad = _pad_heads(w, GLA_HEADS, GLA_DK)
        full = jnp.zeros((LANES, wpad.shape[1]), F32).at[lane0:lane0 + GLA_GATE_RANK].set(wpad)
        return full.astype(BF16), _pad_heads(b.reshape(1, -1), GLA_HEADS, GLA_DK)

    gfw, gfb = gate_w(gate_fwd_w, gate_fwd_b, _M_GF)
    gbw, gbb = gate_w(gate_bwd_w, gate_bwd_b, _M_GB)

    half = MLA_ROPE // 2
    freqs = ROPE_THETA ** (-jnp.arange(half, dtype=F32) * 2.0 / MLA_ROPE)
    ang = jnp.arange(seq, dtype=F32)[:, None] * freqs[None, :]
    cos, sin = jnp.cos(ang), jnp.sin(ang)
    zpad = jnp.zeros((seq, HEAD_PAD - MLA_NOPE - MLA_ROPE), F32)
    ck = jnp.concatenate([jnp.ones((seq, MLA_NOPE), F32), cos, cos, zpad], axis=1)
    sk = jnp.concatenate([jnp.zeros((seq, MLA_NOPE), F32), sin, sin, zpad], axis=1)
    scale = (MLA_NOPE + MLA_ROPE) ** -0.5 * 1.4426950408889634

    woa = w_o[:MLA_HEADS * MLA_V].reshape(MLA_HEADS, MLA_V, D_MODEL)
    woa = jnp.pad(woa, ((0, 0), (0, HEAD_PAD - MLA_V), (0, 0))).reshape(MLA_HEADS * HEAD_PAD, D_MODEL)

    return dict(
        g1=row(norm1_g), w1=w1.astype(BF16), gq=row(q_norm_g), gkv=row(kv_norm_g),
        wqa=wqa.astype(BF16), wqb=wqb.astype(BF16), wka=wka.astype(BF16), wv=wv.astype(BF16),
        pa=pa, pb=pb, gfw=gfw, gfb=gfb, gbw=gbw, gbb=gbb,
        cq=ck * scale, sq=sk * scale, ck=ck, sk=sk,
        gn=row(gla_norm_g), woa=woa.astype(BF16), wob=w_o[MLA_HEADS * MLA_V:].astype(BF16),
        g2=row(norm2_g), wqt=peer_wq.T.astype(BF16),
        subk=peer_subkeys.reshape(2 * PEER_HEADS, PEER_NKEYS, PEER_QDIM // 2).astype(BF16),
        u=peer_u.astype(BF16), vt=peer_v.T.astype(BF16), gfin=row(final_norm_g),
    )


def _tile(n, want):
    t = min(n, want)
    assert n % t == 0, (n, want)
    return t


def _trunk(x, wp):
    b, seq, d = x.shape
    x2 = x.reshape(b * seq, d)
    n = b * seq
    tm = _tile(seq, 512)
    q, k, v, gq, gk, gv, gf, gb, sg = _proj(x2, wp, seq, tm)
    attn = _attn(q, k, v, seq, _tile(seq, 2048))
    ogla = _gla(gq, gk, gv, gf, gb, sg, wp["gn"], seq)
    h, xn = _mix(x2, attn, ogla, wp, tm)
    sel = _select(xn, wp, _tile(n, 256))
    y = _peer(xn, h, sel, wp, _tile(n, 1024), 1024)
    return y.reshape(b, seq, d)


def kernel(x_prompt, x_sample, norm1_g, w_in, q_norm_g, w_uq, kv_norm_g, w_ukv, gate_fwd_w, gate_fwd_b,
           gate_bwd_w, gate_bwd_b, gla_norm_g, w_o, norm2_g, peer_wq, peer_subkeys, peer_u, peer_v,
           final_norm_g):
    assert norm1_g.shape[0] == 1, "single layer trunk"
    assert x_prompt.shape[1] == x_sample.shape[1]
    wp = _prep(x_prompt.shape[1], norm1_g[0], w_in[0], q_norm_g[0], w_uq[0], kv_norm_g[0], w_ukv[0],
               gate_fwd_w[0], gate_fwd_b[0], gate_bwd_w[0], gate_bwd_b[0], gla_norm_g[0], w_o[0],
               norm2_g[0], peer_wq[0], peer_subkeys[0], peer_u[0], peer_v[0], final_norm_g)
    return _trunk(x_prompt, wp), _trunk(x_sample, wp)
```

```python
import functools

import jax
import jax.numpy as jnp
from jax import lax
from jax.experimental import pallas as pl
from jax.experimental.pallas import tpu as pltpu

D_MODEL = 1024
MLA_HEADS = 8
MLA_Q_LORA = 256
MLA_KV_LORA = 128
MLA_NOPE = 64
MLA_ROPE = 32
MLA_V = 64
ROPE_THETA = 10000.0
GLA_HEADS = 4
GLA_DK = 64
GLA_DV = 128
GLA_GATE_RANK = 16
GLA_GATE_NORM = 16.0
GLA_CHUNK = 64
GLA_GROUP = 4
PEER_HEADS = 8
PEER_NKEYS = 128
PEER_NEXP = PEER_NKEYS * PEER_NKEYS
PEER_QDIM = 256
PEER_TOPK = 16
EPS = 1e-6

LANES = 128
HEAD_PAD = LANES
VMEM_LIMIT_BYTES = 56 * 1024 * 1024

_C_Q = 0
_C_KV = _C_Q + MLA_Q_LORA
_C_GQ = _C_KV + MLA_KV_LORA
_C_GK = _C_GQ + GLA_HEADS * HEAD_PAD
_C_GV = _C_GK + GLA_HEADS * HEAD_PAD
_C_GO = _C_GV + GLA_HEADS * GLA_DV
_C_MISC = _C_GO + GLA_HEADS * GLA_DV
_IN_COLS = _C_MISC + LANES
_M_KR = 0
_M_KROT = MLA_ROPE
_M_GF = 2 * MLA_ROPE
_M_GB = _M_GF + GLA_GATE_RANK

BF16 = jnp.bfloat16
F32 = jnp.float32
PACK = 16


def _dot(a, b):
    return jnp.dot(a, b, preferred_element_type=F32)


def _dot_nt(a, b):
    return lax.dot_general(a, b, (((1,), (1,)), ((), ())), preferred_element_type=F32)


def _dot_tn(a, b):
    return lax.dot_general(a, b, (((0,), (0,)), ((), ())), preferred_element_type=F32)


def _rms(x, g):
    return x * lax.rsqrt(jnp.mean(x * x, axis=-1, keepdims=True) + EPS) * g


def _log_sigmoid(x):
    return jnp.minimum(x, 0.0) - jnp.log(1.0 + jnp.exp(-jnp.abs(x)))


def _gelu_tanh(x):
    k = -2.0 * 0.7978845608028654 * 1.4426950408889634
    return x * (1.0 / (1.0 + jnp.exp2(x * (k + (k * 0.044715) * (x * x)))))


def _params(*sem):
    return pltpu.CompilerParams(dimension_semantics=sem, vmem_limit_bytes=VMEM_LIMIT_BYTES)


def _full(shape):
    nd = len(shape)
    return pl.BlockSpec(shape, lambda *_: (0,) * nd)


def _proj_kernel(x_ref, g1_ref, w1_ref, gq_ref, gkv_ref, wqa_ref, wqb_ref, wka_ref, wv_ref,
                 pa_ref, pb_ref, gfw_ref, gfb_ref, gbw_ref, gbb_ref,
                 cq_ref, sq_ref, ck_ref, sk_ref,
                 q_ref, k_ref, v_ref, gq_o, gk_o, gv_o, gf_o, gb_o, sg_o):
    x = x_ref[...]
    n1 = _rms(x, g1_ref[...]).astype(BF16)
    p = _dot(n1, w1_ref[...])
    cqn = _rms(p[:, _C_Q:_C_Q + MLA_Q_LORA], gq_ref[...]).astype(BF16)
    ckvn = _rms(p[:, _C_KV:_C_KV + MLA_KV_LORA], gkv_ref[...]).astype(BF16)
    misc = p[:, _C_MISC:_C_MISC + LANES].astype(BF16)

    qa = _dot(cqn, wqa_ref[...])
    qb = _dot(cqn, wqb_ref[...])
    ka = _dot(ckvn, wka_ref[...]) + _dot(misc, pa_ref[...])
    kb = _dot(misc, pb_ref[...])
    cq, sq, ck, sk = cq_ref[...], sq_ref[...], ck_ref[...], sk_ref[...]
    for h in range(MLA_HEADS):
        sl = slice(h * HEAD_PAD, (h + 1) * HEAD_PAD)
        q_ref[:, sl] = (qa[:, sl] * cq + qb[:, sl] * sq).astype(BF16)
        k_ref[:, sl] = (ka[:, sl] * ck + kb[:, sl] * sk).astype(BF16)
    lane = lax.broadcasted_iota(jnp.int32, (1, MLA_HEADS * HEAD_PAD), 1)
    ones_lane = jnp.where(lane % HEAD_PAD == MLA_V, 1.0, 0.0)
    v_ref[...] = (_dot(ckvn, wv_ref[...]) + ones_lane).astype(BF16)

    w = GLA_HEADS * HEAD_PAD
    gq_o[...] = p[:, _C_GQ:_C_GQ + w].astype(BF16)
    gk_o[...] = p[:, _C_GK:_C_GK + w].astype(BF16)
    gv_o[...] = p[:, _C_GV:_C_GV + GLA_HEADS * GLA_DV].astype(BF16)
    go = p[:, _C_GO:_C_GO + GLA_HEADS * GLA_DV]
    sg_o[...] = (go * (1.0 / (1.0 + jnp.exp(-go)))).astype(BF16)
    gf_o[...] = _log_sigmoid(_dot(misc, gfw_ref[...]) + gfb_ref[...]) * (1.0 / GLA_GATE_NORM)
    gb_o[...] = _log_sigmoid(_dot(misc, gbw_ref[...]) + gbb_ref[...]) * (1.0 / GLA_GATE_NORM)


def _proj(x2, wp, seq, tm):
    n = x2.shape[0]
    tiles_per_seq = seq // tm
    tok = lambda w: pl.BlockSpec((tm, w), lambda i: (i, 0))
    tab = pl.BlockSpec((tm, HEAD_PAD), lambda i: (i % tiles_per_seq, 0))
    weights = [wp["g1"], wp["w1"], wp["gq"], wp["gkv"], wp["wqa"], wp["wqb"], wp["wka"], wp["wv"],
               wp["pa"], wp["pb"], wp["gfw"], wp["gfb"], wp["gbw"], wp["gbb"]]
    tables = [wp["cq"], wp["sq"], wp["ck"], wp["sk"]]
    hp = MLA_HEADS * HEAD_PAD
    gw = GLA_HEADS * HEAD_PAD
    gv = GLA_HEADS * GLA_DV
    out_w = [(hp, BF16), (hp, BF16), (hp, BF16), (gw, BF16), (gw, BF16), (gv, BF16),
             (gw, F32), (gw, F32), (gv, BF16)]
    return pl.pallas_call(
        _proj_kernel,
        grid=(n // tm,),
        in_specs=[tok(D_MODEL)] + [_full(w.shape) for w in weights] + [tab] * 4,
        out_specs=[tok(w) for w, _ in out_w],
        out_shape=[jax.ShapeDtypeStruct((n, w), dt) for w, dt in out_w],
        compiler_params=_params("parallel"),
        name="proj",
    )(x2, *weights, *tables)


ATTN_SUB = 512


def _attn_kernel(q_ref, k_ref, v_ref, o_ref):
    k = k_ref[...]
    v = v_ref[...]
    subs = [slice(i * ATTN_SUB, (i + 1) * ATTN_SUB) for i in range(q_ref.shape[0] // ATTN_SUB)]
    scores = [_dot_nt(q_ref[r, :], k) for r in subs]
    for r, s in zip(subs, scores):
        p = jnp.exp2(s - jnp.max(s, axis=-1, keepdims=True)).astype(BF16)
        o = _dot(p, v)
        o_ref[r, :] = (o * (1.0 / o[:, MLA_V:MLA_V + 1])).astype(BF16)


def _attn(q, k, v, seq, tq):
    n = q.shape[0]
    nb = n // seq
    nq = seq // tq
    return pl.pallas_call(
        _attn_kernel,
        grid=(nb, MLA_HEADS, nq),
        in_specs=[pl.BlockSpec((tq, HEAD_PAD), lambda b, h, i: (b * nq + i, h)),
                  pl.BlockSpec((seq, HEAD_PAD), lambda b, h, i: (b, h)),
                  pl.BlockSpec((seq, HEAD_PAD), lambda b, h, i: (b, h))],
        out_specs=pl.BlockSpec((tq, HEAD_PAD), lambda b, h, i: (b * nq + i, h)),
        out_shape=jax.ShapeDtypeStruct((n, MLA_HEADS * HEAD_PAD), BF16),
        compiler_params=_params("parallel", "parallel", "arbitrary"),
        name="attn",
    )(q, k, v)


GLA_HEADS_PER_STEP = 2


def _gla_kernel(q_ref, k_ref, v_ref, gf_ref, gb_ref, sg_ref, gn_ref, o_ref, of_ref, ob_ref, *, seq):
    c = GLA_CHUNK
    span = c * GLA_GROUP
    ntrip = seq // span
    row = lax.broadcasted_iota(jnp.int32, (span, span), 0)
    col = lax.broadcasted_iota(jnp.int32, (span, span), 1)
    same_chunk = (row // c) == (col // c)
    causal_f = same_chunk & (row >= col)
    causal_b = same_chunk & (row <= col)
    heads = [slice(i * HEAD_PAD, (i + 1) * HEAD_PAD) for i in range(GLA_HEADS_PER_STEP)]

    def stage_local(rows, head, g_ref, causal, edge):
        q = q_ref[rows, head].astype(F32)
        k = k_ref[rows, head].astype(F32)
        v = v_ref[rows, head]
        g = g_ref[rows, head]
        tri = jnp.where(causal, 1.0, 0.0).astype(BF16)
        hi = g.astype(BF16)
        lo = (g - hi.astype(F32)).astype(BF16)
        b = _dot(tri, hi) + _dot(tri, lo)
        edges = [b[j * c + edge:j * c + edge + 1, :] for j in range(GLA_GROUP)]
        b_edge = jnp.concatenate([jnp.broadcast_to(e, (c, HEAD_PAD)) for e in edges], axis=0)
        q_dec = (q * jnp.exp(b)).astype(BF16)
        k_inc = (k * jnp.exp(-b)).astype(BF16)
        k_dec = (k * jnp.exp(b_edge - b)).astype(BF16)
        att = jnp.where(causal, _dot_nt(q_dec, k_inc), 0.0).astype(BF16)
        o_intra = _dot(att, v)
        kv_t = [_dot_tn(v[j * c:(j + 1) * c, :], k_dec[j * c:(j + 1) * c, :]) for j in range(GLA_GROUP)]
        decay = [jnp.exp(e) for e in edges]
        return q_dec, o_intra, kv_t, decay

    def stage_state(order, local, state_t):
        q_dec, o_intra, kv_t, decay = local
        outs = [None] * GLA_GROUP
        for j in order:
            rows = slice(j * c, (j + 1) * c)
            outs[j] = o_intra[rows, :] + _dot_nt(q_dec[rows, :], state_t.astype(BF16))
            state_t = state_t * decay[j] + kv_t[j]
        return jnp.concatenate(outs, axis=0), state_t

    def body(i, carry):
        rf = pl.ds(pl.multiple_of(i * span, span), span)
        rb = pl.ds(pl.multiple_of((ntrip - 1 - i) * span, span), span)
        local_f = [stage_local(rf, hd, gf_ref, causal_f, c - 1) for hd in heads]
        local_b = [stage_local(rb, hd, gb_ref, causal_b, 0) for hd in heads]
        new = []
        for n, hd in enumerate(heads):
            o_f, sf = stage_state(range(GLA_GROUP), local_f[n], carry[2 * n])
            o_b, sb = stage_state(reversed(range(GLA_GROUP)), local_b[n], carry[2 * n + 1])
            of_ref[rf, hd] = o_f
            ob_ref[rb, hd] = o_b
            new += [sf, sb]
        return tuple(new)

    zero = jnp.zeros((GLA_DV, HEAD_PAD), F32)
    lax.fori_loop(0, ntrip, body, (zero,) * (2 * GLA_HEADS_PER_STEP))
    for hd in heads:
        o = of_ref[:, hd] + ob_ref[:, hd]
        o_ref[:, hd] = (_rms(o, gn_ref[...]) * sg_ref[:, hd].astype(F32)).astype(BF16)


def _gla(gq, gk, gv, gf, gb, sg, gn, seq):
    n = gq.shape[0]
    nb = n // seq
    assert HEAD_PAD == GLA_DV
    width = GLA_HEADS_PER_STEP * GLA_DV
    blk = pl.BlockSpec((seq, width), lambda b, h: (b, h))
    return pl.pallas_call(
        functools.partial(_gla_kernel, seq=seq),
        grid=(nb, GLA_HEADS // GLA_HEADS_PER_STEP),
        in_specs=[blk] * 6 + [_full(gn.shape)],
        out_specs=blk,
        out_shape=jax.ShapeDtypeStruct((n, GLA_HEADS * GLA_DV), BF16),
        scratch_shapes=[pltpu.VMEM((seq, width), F32), pltpu.VMEM((seq, width), F32)],
        compiler_params=_params("parallel", "parallel"),
        name="gla",
    )(gq, gk, gv, gf, gb, sg, gn)


def _mix_kernel(x_ref, a_ref, g_ref, woa_ref, wob_ref, g2_ref, h_ref, xn_ref):
    h = x_ref[...] + _dot(a_ref[...], woa_ref[...]) + _dot(g_ref[...], wob_ref[...])
    h_ref[...] = h
    xn_ref[...] = _rms(h, g2_ref[...]).astype(BF16)


def _mix(x2, attn, ogla, wp, tm):
    n = x2.shape[0]
    tok = lambda w: pl.BlockSpec((tm, w), lambda i: (i, 0))
    return pl.pallas_call(
        _mix_kernel,
        grid=(n // tm,),
        in_specs=[tok(D_MODEL), tok(attn.shape[1]), tok(ogla.shape[1]),
                  _full(wp["woa"].shape), _full(wp["wob"].shape), _full(wp["g2"].shape)],
        out_specs=[tok(D_MODEL), tok(D_MODEL)],
        out_shape=[jax.ShapeDtypeStruct((n, D_MODEL), F32), jax.ShapeDtypeStruct((n, D_MODEL), BF16)],
        compiler_params=_params("parallel"),
        name="mix",
    )(x2, attn, ogla, wp["woa"], wp["wob"], wp["g2"])


def _sort_pairs(n):
    pairs = []
    p = 1
    while p < n:
        k = p
        while k >= 1:
            for j in range(k % p, n - k, 2 * k):
                for i in range(min(k, n - j - k)):
                    if (i + j) // (2 * p) == (i + j + k) // (2 * p):
                        pairs.append((i + j, i + j + k))
            k //= 2
        p *= 2
    return pairs


def _top_values(s, count):
    sub = 8
    groups = s.shape[0] // (sub * count)
    v = [s[sub * i:sub * (i + 1), :] for i in range(groups * count)]

    def exchange(i, j):
        v[i], v[j] = jnp.maximum(v[i], v[j]), jnp.minimum(v[i], v[j])

    def merge_halves(lo):
        d = count // 2
        while d >= 1:
            for k in range(count):
                if k & d == 0:
                    exchange(lo + k, lo + k + d)
            d //= 2

    for g in range(groups):
        for i, j in _sort_pairs(count):
            exchange(g * count + i, g * count + j)
    while groups > 1:
        groups //= 2
        for g in range(groups):
            a, b = 2 * g * count, (2 * g + 1) * count
            for k in range(count):
                v[g * count + k] = jnp.maximum(v[a + k], v[b + count - 1 - k])
            merge_halves(g * count)
    shift = sub // 2
    while shift >= 1:
        other = [pltpu.roll(v[count - 1 - k], shift, axis=0) for k in range(count)]
        for k in range(count):
            v[k] = jnp.maximum(v[k], other[k])
        merge_halves(0)
        shift //= 2
    return [v[k][0:1, :] for k in range(count)]


def _packed_row(row):
    return jnp.broadcast_to(row[None, :], (PACK, row.shape[0])).astype(BF16)


def _select_kernel(xn_ref, wqt_ref, sk_ref, r2_ref, e2_ref, e1_ref, cnt_ref):
    qt = _dot_nt(wqt_ref[...], xn_ref[...])
    half = PEER_QDIM // 2
    for h in range(PEER_HEADS):
        s, tops = [], []
        for p in range(2):
            r0 = (2 * h + p) * half
            sp = _dot(sk_ref[2 * h + p], qt[r0:r0 + half, :].astype(BF16))
            s.append(sp)
            tops.append(_top_values(sp, PEER_TOPK))
        top1 = jnp.concatenate(tops[0], axis=0)
        top2 = jnp.concatenate(tops[1], axis=0)
        oct_ = PEER_TOPK // 2
        sub = lax.broadcasted_iota(jnp.int32, (oct_, top2.shape[1]), 0)
        rows = [tops[0][0] + top2]
        for a in range(1, oct_):
            rows.append(jnp.where(sub < PEER_TOPK // (a + 1), tops[0][a] + top2[:oct_, :], -jnp.inf))
        tail = top1[oct_:, :] + tops[1][0]
        cand = jnp.concatenate(rows + [tail], axis=0)
        pad = jnp.full((-cand.shape[0] % (8 * PEER_TOPK), cand.shape[1]), -jnp.inf, F32)
        thr = _top_values(jnp.concatenate([cand, pad], axis=0), PEER_TOPK)[-1]
        best = tops[0][0] + tops[1][0]
        z = jnp.sum(jnp.where(cand >= thr, jnp.exp(cand - best), 0.0), axis=0, keepdims=True)
        counts = [jnp.sum(jnp.where(r >= thr, 1.0, 0.0), axis=0, keepdims=True) for r in rows]
        tail_hit = jnp.where(tail >= thr, 1.0, 0.0)
        counts += [tail_hit[a:a + 1, :] for a in range(PEER_TOPK - oct_)]
        cnt = jnp.zeros_like(s[0])
        for a in reversed(range(PEER_TOPK)):
            cnt = jnp.where(s[0] >= tops[0][a], counts[a], cnt)
        rank2 = jnp.zeros_like(s[1])
        for a in range(PEER_TOPK):
            rank2 = jnp.where(s[1] < tops[1][a], a + 1.0, rank2)
        r2_ref[h] = rank2.astype(BF16)
        e2_ref[h] = (jnp.exp(s[1] - tops[1][0]) * (1.0 / z)).astype(BF16)
        e1_ref[h] = jnp.exp(s[0] - tops[0][0])
        cnt_ref[h] = cnt


def _select(xn, wp, tb):
    n = xn.shape[0]
    blk = pl.BlockSpec((PEER_HEADS, PEER_NKEYS, tb), lambda i: (0, 0, i))
    shp = lambda dt: jax.ShapeDtypeStruct((PEER_HEADS, PEER_NKEYS, n), dt)
    return pl.pallas_call(
        _select_kernel,
        grid=(n // tb,),
        in_specs=[pl.BlockSpec((tb, D_MODEL), lambda i: (i, 0)),
                  _full(wp["wqt"].shape), _full(wp["subk"].shape)],
        out_specs=[blk, blk, blk, blk],
        out_shape=[shp(BF16), shp(BF16), shp(F32), shp(F32)],
        compiler_params=_params("parallel"),
        name="peer_select",
    )(xn, wp["wqt"], wp["subk"])


def _peer_kernel(xn_ref, h_ref, r2_ref, e2_ref, e1_ref, cnt_ref, u_ref, vt_ref, gfin_ref,
                 o_ref, acc_ref, g_ref, *, rows_per_chunk):
    c = pl.program_id(1)

    @pl.when(c == 0)
    def _():
        acc_ref[...] = jnp.zeros_like(acc_ref)

    act = _gelu_tanh(_dot_nt(u_ref[...], xn_ref[...]).astype(BF16))

    for j in range(rows_per_chunk):
        e1 = [_packed_row(e1_ref[h, j, :]) for h in range(PEER_HEADS)]
        cnt = [_packed_row(cnt_ref[h, j, :]) for h in range(PEER_HEADS)]
        for m in range(PEER_NKEYS // PACK):
            keys = slice(m * PACK, (m + 1) * PACK)
            w = None
            for h in range(PEER_HEADS):
                term = jnp.where(r2_ref[h, keys, :] < cnt[h], e2_ref[h, keys, :], 0.0) * e1[h]
                w = term if w is None else w + term
            rows = slice(j * PEER_NKEYS + m * PACK, j * PEER_NKEYS + (m + 1) * PACK)
            g_ref[rows, :] = act[rows, :] * w
    acc_ref[...] += _dot(vt_ref[...], g_ref[...])

    @pl.when(c == pl.num_programs(1) - 1)
    def _():
        y = h_ref[...] + acc_ref[...].T
        o_ref[...] = _rms(y, gfin_ref[...])


def _peer(xn, h, sel, wp, tb, ec):
    n = xn.shape[0]
    r2, e2, e1, cnt = sel
    rpc = ec // PEER_NKEYS
    tok = lambda: pl.BlockSpec((tb, D_MODEL), lambda i, c: (i, 0))
    part = pl.BlockSpec((PEER_HEADS, rpc, tb), lambda i, c: (0, c, i))
    whole = pl.BlockSpec((PEER_HEADS, PEER_NKEYS, tb), lambda i, c: (0, 0, i))
    return pl.pallas_call(
        functools.partial(_peer_kernel, rows_per_chunk=rpc),
        grid=(n // tb, PEER_NEXP // ec),
        in_specs=[tok(), tok(), whole, whole, part, part,
                  pl.BlockSpec((ec, D_MODEL), lambda i, c: (c, 0)),
                  pl.BlockSpec((D_MODEL, ec), lambda i, c: (0, c)),
                  _full(wp["gfin"].shape)],
        out_specs=tok(),
        out_shape=jax.ShapeDtypeStruct((n, D_MODEL), F32),
        scratch_shapes=[pltpu.VMEM((D_MODEL, tb), F32), pltpu.VMEM((ec, tb), BF16)],
        compiler_params=_params("parallel", "arbitrary"),
        name="peer_dense",
    )(xn, h, r2, e2, e1, cnt, wp["u"], wp["vt"], wp["gfin"])


def _pad_heads(w, heads, width):
    r = w.shape[0]
    w = w.reshape(r, heads, width)
    return jnp.pad(w, ((0, 0), (0, 0), (0, HEAD_PAD - width))).reshape(r, heads * HEAD_PAD)


def _rot_cols(w):
    half = MLA_ROPE // 2
    return jnp.concatenate([-w[..., half:], w[..., :half]], axis=-1)


def _prep(seq, norm1_g, w_in, q_norm_g, w_uq, kv_norm_g, w_ukv, gate_fwd_w, gate_fwd_b,
          gate_bwd_w, gate_bwd_b, gla_norm_g, w_o, norm2_g, peer_wq, peer_subkeys, peer_u, peer_v,
          final_norm_g):
    row = lambda g: g.reshape(1, -1).astype(F32)
    o = 0
    segs = {}
    for name, width in (("cq", MLA_Q_LORA), ("ckv", MLA_KV_LORA), ("kr", MLA_ROPE),
                        ("gq", GLA_HEADS * GLA_DK), ("gk", GLA_HEADS * GLA_DK), ("gv", GLA_HEADS * GLA_DV),
                        ("lf", GLA_GATE_RANK), ("lb", GLA_GATE_RANK), ("go", GLA_HEADS * GLA_DV)):
        segs[name] = w_in[:, o:o + width]
        o += width
    misc = jnp.concatenate([segs["kr"], _rot_cols(segs["kr"]), segs["lf"], segs["lb"],
                            jnp.zeros((D_MODEL, LANES - _M_GB - GLA_GATE_RANK), F32)], axis=1)
    w1 = jnp.concatenate([segs["cq"], segs["ckv"],
                          _pad_heads(segs["gq"] * (GLA_DK ** -0.5), GLA_HEADS, GLA_DK),
                          _pad_heads(segs["gk"], GLA_HEADS, GLA_DK),
                          segs["gv"], segs["go"], misc], axis=1)

    wq = w_uq.reshape(MLA_Q_LORA, MLA_HEADS, MLA_NOPE + MLA_ROPE)
    zq = jnp.zeros((MLA_Q_LORA, MLA_HEADS, HEAD_PAD - MLA_NOPE - MLA_ROPE), F32)
    wqa = jnp.concatenate([wq, zq], axis=-1).reshape(MLA_Q_LORA, -1)
    wqb = jnp.concatenate([jnp.zeros_like(wq[..., :MLA_NOPE]), _rot_cols(wq[..., MLA_NOPE:]), zq],
                          axis=-1).reshape(MLA_Q_LORA, -1)
    wkv = w_ukv.reshape(MLA_KV_LORA, MLA_HEADS, MLA_NOPE + MLA_V)
    wka = _pad_heads(wkv[..., :MLA_NOPE].reshape(MLA_KV_LORA, -1), MLA_HEADS, MLA_NOPE)
    wv = _pad_heads(wkv[..., MLA_NOPE:].reshape(MLA_KV_LORA, -1), MLA_HEADS, MLA_V)

    lane = jnp.arange(MLA_HEADS * HEAD_PAD) % HEAD_PAD
    src = jnp.arange(LANES)[:, None]
    in_rope = (lane >= MLA_NOPE) & (lane < MLA_NOPE + MLA_ROPE)
    pa = ((src == (lane - MLA_NOPE + _M_KR)[None, :]) & in_rope[None, :]).astype(BF16)
    pb = ((src == (lane - MLA_NOPE + _M_KROT)[None, :]) & in_rope[None, :]).astype(BF16)

    def gate_w(w, b, lane0):
        wpad = _pad_heads(w, GLA_HEADS, GLA_DK)
        full = jnp.zeros((LANES, wpad.shape[1]), F32).at[lane0:lane0 + GLA_GATE_RANK].set(wpad)
        return full.astype(BF16), _pad_heads(b.reshape(1, -1), GLA_HEADS, GLA_DK)

    gfw, gfb = gate_w(gate_fwd_w, gate_fwd_b, _M_GF)
    gbw, gbb = gate_w(gate_bwd_w, gate_bwd_b, _M_GB)

    half = MLA_ROPE // 2
    freqs = ROPE_THETA ** (-jnp.arange(half, dtype=F32) * 2.0 / MLA_ROPE)
    ang = jnp.arange(seq, dtype=F32)[:, None] * freqs[None, :]
    cos, sin = jnp.cos(ang), jnp.sin(ang)
    zpad = jnp.zeros((seq, HEAD_PAD - MLA_NOPE - MLA_ROPE), F32)
    ck = jnp.concatenate([jnp.ones((seq, MLA_NOPE), F32), cos, cos, zpad], axis=1)
    sk = jnp.concatenate([jnp.zeros((seq, MLA_NOPE), F32), sin, sin, zpad], axis=1)
    scale = (MLA_NOPE + MLA_ROPE) ** -0.5 * 1.4426950408889634

    woa = w_o[:MLA_HEADS * MLA_V].reshape(MLA_HEADS, MLA_V, D_MODEL)
    woa = jnp.pad(woa, ((0, 0), (0, HEAD_PAD - MLA_V), (0, 0))).reshape(MLA_HEADS * HEAD_PAD, D_MODEL)

    return dict(
        g1=row(norm1_g), w1=w1.astype(BF16), gq=row(q_norm_g), gkv=row(kv_norm_g),
        wqa=wqa.astype(BF16), wqb=wqb.astype(BF16), wka=wka.astype(BF16), wv=wv.astype(BF16),
        pa=pa, pb=pb, gfw=gfw, gfb=gfb, gbw=gbw, gbb=gbb,
        cq=ck * scale, sq=sk * scale, ck=ck, sk=sk,
        gn=row(gla_norm_g), woa=woa.astype(BF16), wob=w_o[MLA_HEADS * MLA_V:].astype(BF16),
        g2=row(norm2_g), wqt=peer_wq.T.astype(BF16),
        subk=peer_subkeys.reshape(2 * PEER_HEADS, PEER_NKEYS, PEER_QDIM // 2).astype(BF16),
        u=peer_u.astype(BF16), vt=peer_v.T.astype(BF16), gfin=row(final_norm_g),
    )


def _tile(n, want):
    t = min(n, want)
    assert n % t == 0, (n, want)
    return t


PROJ_TILE = 512
ATTN_QUERY_TILE = 2048
SELECT_TILE = 512
PEER_TOKEN_TILE = 1024
PEER_EXPERT_CHUNK = 1024


def _trunk(x, wp):
    b, seq, d = x.shape
    x2 = x.reshape(b * seq, d)
    n = b * seq
    tm = _tile(seq, PROJ_TILE)
    q, k, v, gq, gk, gv, gf, gb, sg = _proj(x2, wp, seq, tm)
    attn = _attn(q, k, v, seq, _tile(seq, ATTN_QUERY_TILE))
    ogla = _gla(gq, gk, gv, gf, gb, sg, wp["gn"], seq)
    h, xn = _mix(x2, attn, ogla, wp, tm)
    sel = _select(xn, wp, _tile(n, SELECT_TILE))
    y = _peer(xn, h, sel, wp, _tile(n, PEER_TOKEN_TILE), PEER_EXPERT_CHUNK)
    return y.reshape(b, seq, d)


def kernel(x_prompt, x_sample, norm1_g, w_in, q_norm_g, w_uq, kv_norm_g, w_ukv, gate_fwd_w, gate_fwd_b,
           gate_bwd_w, gate_bwd_b, gla_norm_g, w_o, norm2_g, peer_wq, peer_subkeys, peer_u, peer_v,
           final_norm_g):
    assert norm1_g.shape[0] == 1, "single layer trunk"
    assert x_prompt.shape[1] == x_sample.shape[1]
    wp = _prep(x_prompt.shape[1], norm1_g[0], w_in[0], q_norm_g[0], w_uq[0], kv_norm_g[0], w_ukv[0],
               gate_fwd_w[0], gate_fwd_b[0], gate_bwd_w[0], gate_bwd_b[0], gla_norm_g[0], w_o[0],
               norm2_g[0], peer_wq[0], peer_subkeys[0], peer_u[0], peer_v[0], final_norm_g)
    return _trunk(x_prompt, wp), _trunk(x_sample, wp)
```
